```python
import math
import jax, jax.numpy as jnp
from jax import lax
import numpy as np

D_MODEL = 1024
BATCH = 2
SEQ = 8192
DEPTH = 2

N_META = 16
N_A = DEPTH // 2
N_B = DEPTH - N_A
N_HEADS = 8
HEAD_DIM = D_MODEL // N_HEADS
D_FF = 2816
CONV_WIDTH = 3
BLOCK = 128
PAD = (-N_META) % BLOCK
LN_EPS = 1e-5
DEEPNORM_ALPHA = (2 * DEPTH) ** 0.25
DEEPNORM_BETA = (8 * DEPTH) ** -0.25
NEG_INF = -1e30

kernel_name = "yoco_shortconv_fox_macaron_deepnorm"


def layer_norm(x, g, b):
    xf = x.astype(jnp.float32)
    mu = jnp.mean(xf, axis=-1, keepdims=True)
    var = jnp.mean(jnp.square(xf - mu), axis=-1, keepdims=True)
    y = (xf - mu) * lax.rsqrt(var + LN_EPS) * g.astype(jnp.float32) + b.astype(jnp.float32)
    return y.astype(x.dtype)


def swiglu(x, wg, wu, wd):
    return (jax.nn.silu(x @ wg) * (x @ wu)) @ wd


def short_conv(x, w_in, w_conv, w_out):
    bgate, cgate, val = jnp.split(x @ w_in, 3, axis=-1)
    u = cgate * val
    y = lax.conv_general_dilated(
        u, w_conv[:, None, :].astype(u.dtype),
        window_strides=(1,), padding=[(CONV_WIDTH - 1, 0)],
        dimension_numbers=("NWC", "WIO", "NWC"),
        feature_group_count=D_MODEL)
    return (bgate * y) @ w_out


def shared_kv(h, kv_w, f_bias):
    bsz, L, _ = h.shape
    kvf = h @ kv_w
    k = kvf[..., :D_MODEL].reshape(bsz, L, N_HEADS, HEAD_DIM)
    v = kvf[..., D_MODEL:2 * D_MODEL].reshape(bsz, L, N_HEADS, HEAD_DIM)
    f_logit = (kvf[..., 2 * D_MODEL:] + f_bias).astype(jnp.float32)
    log_f = jax.nn.log_sigmoid(f_logit)
    pad4 = ((0, 0), (PAD, 0), (0, 0), (0, 0))
    k = jnp.pad(k, pad4).transpose(0, 2, 1, 3)
    v = jnp.pad(v, pad4).transpose(0, 2, 1, 3)
    log_f = jnp.pad(log_f, ((0, 0), (PAD, 0), (0, 0)))
    c = jnp.cumsum(log_f, axis=1).transpose(0, 2, 1)
    return k, v, c


def forgetting_attention(h, w_q, w_o, k, v, c):
    bsz, L, _ = h.shape
    Lp = L + PAD
    n_blocks = Lp // BLOCK
    scale = 1.0 / math.sqrt(HEAD_DIM)
    q = (h @ w_q).reshape(bsz, L, N_HEADS, HEAD_DIM)
    q = jnp.pad(q, ((0, 0), (PAD, 0), (0, 0), (0, 0))).transpose(0, 2, 1, 3)
    k_pos = jnp.arange(Lp)

    def one_block(i):
        start = i * BLOCK
        qb = lax.dynamic_slice_in_dim(q, start, BLOCK, axis=2)
        cq = lax.dynamic_slice_in_dim(c, start, BLOCK, axis=2)
        s = jnp.einsum("bhqd,bhkd->bhqk", qb, k).astype(jnp.float32) * scale
        s = s + cq[..., :, None] - c[..., None, :]
        q_pos = start + jnp.arange(BLOCK)
        mask = (k_pos[None, :] <= q_pos[:, None]) & (k_pos[None, :] >= PAD)
        s = jnp.where(mask, s, NEG_INF)
        p = jax.nn.softmax(s, axis=-1)
        return jnp.einsum("bhqk,bhkd->bhqd", p.astype(v.dtype), v)

    o = lax.map(one_block, jnp.arange(n_blocks))
    o = o.transpose(1, 0, 3, 2, 4).reshape(bsz, Lp, D_MODEL)[:, PAD:]
    return o @ w_o


def setup_inputs(seed: int = 0) -> dict:
    key = jax.random.key(seed)
    ks = jax.random.split(key, 20)
    f32 = jnp.float32

    def nrm(k, shape, scale):
        return jax.random.normal(k, shape, f32) * scale

    d_s = D_MODEL ** -0.5
    f_s = D_FF ** -0.5
    x = nrm(ks[0], (BATCH, SEQ, D_MODEL), 1.0)
    meta = nrm(ks[1], (N_META, D_MODEL), 1.0)
    ffn1_wg = nrm(ks[2], (DEPTH, D_MODEL, D_FF), d_s)
    ffn1_wu = nrm(ks[3], (DEPTH, D_MODEL, D_FF), d_s)
    ffn1_wd = nrm(ks[4], (DEPTH, D_FF, D_MODEL), f_s * DEEPNORM_BETA)
    ffn2_wg = nrm(ks[5], (DEPTH, D_MODEL, D_FF), d_s)
    ffn2_wu = nrm(ks[6], (DEPTH, D_MODEL, D_FF), d_s)
    ffn2_wd = nrm(ks[7], (DEPTH, D_FF, D_MODEL), f_s * DEEPNORM_BETA)
    ln_gain = 1.0 + nrm(ks[8], (DEPTH, 3, D_MODEL), 0.02)
    ln_bias = nrm(ks[9], (DEPTH, 3, D_MODEL), 0.02)
    conv_w_in = nrm(ks[10], (N_A, D_MODEL, 3 * D_MODEL), d_s)
    conv_w = nrm(ks[11], (N_A, CONV_WIDTH, D_MODEL), CONV_WIDTH ** -0.5)
    conv_w_out = nrm(ks[12], (N_A, D_MODEL, D_MODEL), d_s * DEEPNORM_BETA)
    w_k = nrm(ks[13], (D_MODEL, D_MODEL), d_s)
    w_v = nrm(ks[14], (D_MODEL, D_MODEL), d_s * DEEPNORM_BETA)
    w_f = nrm(ks[15], (D_MODEL, N_HEADS), d_s * 0.5)
    kv_w = jnp.concatenate([w_k, w_v, w_f], axis=1)
    f_bias = 3.0 + nrm(ks[16], (N_HEADS,), 0.5)
    attn_w_q = nrm(ks[17], (N_B, D_MODEL, D_MODEL), d_s)
    attn_w_o = nrm(ks[18], (N_B, D_MODEL, D_MODEL), d_s * DEEPNORM_BETA)
    return {"x": x, "meta": meta,
            "ffn1_wg": ffn1_wg, "ffn1_wu": ffn1_wu, "ffn1_wd": ffn1_wd,
            "ffn2_wg": ffn2_wg, "ffn2_wu": ffn2_wu, "ffn2_wd": ffn2_wd,
            "ln_gain": ln_gain, "ln_bias": ln_bias,
            "conv_w_in": conv_w_in, "conv_w": conv_w, "conv_w_out": conv_w_out,
            "kv_w": kv_w, "f_bias": f_bias,
            "attn_w_q": attn_w_q, "attn_w_o": attn_w_o}


def reference(x, meta, ffn1_wg, ffn1_wu, ffn1_wd, ffn2_wg, ffn2_wu, ffn2_wd,
              ln_gain, ln_bias, conv_w_in, conv_w, conv_w_out, kv_w, f_bias,
              attn_w_q, attn_w_o):
    bsz = x.shape[0]
    h = jnp.concatenate(
        [jnp.broadcast_to(meta.astype(x.dtype)[None], (bsz, N_META, D_MODEL)), x], axis=1)
    k_sh = v_sh = c_sh = None
    for l in range(DEPTH):
        h = layer_norm(DEEPNORM_ALPHA * h + 0.5 * swiglu(h, ffn1_wg[l], ffn1_wu[l], ffn1_wd[l]),
                       ln_gain[l, 0], ln_bias[l, 0])
        if l < N_A:
            mix = short_conv(h, conv_w_in[l], conv_w[l], conv_w_out[l])
        else:
            j = l - N_A
            mix = forgetting_attention(h, attn_w_q[j], attn_w_o[j], k_sh, v_sh, c_sh)
        h = layer_norm(DEEPNORM_ALPHA * h + mix, ln_gain[l, 1], ln_bias[l, 1])
        h = layer_norm(DEEPNORM_ALPHA * h + 0.5 * swiglu(h, ffn2_wg[l], ffn2_wu[l], ffn2_wd[l]),
                       ln_gain[l, 2], ln_bias[l, 2])
        if l == N_A - 1:
            k_sh, v_sh, c_sh = shared_kv(h, kv_w, f_bias)
    return h[:, N_META:]
```

```python
import functools
import math

import jax
import jax.numpy as jnp
from jax import lax
from jax.experimental import pallas as pl
from jax.experimental.pallas import tpu as pltpu

F32 = jnp.float32
BF16 = jnp.bfloat16

LN_EPS = 1e-5
MASK_VALUE = 1e30
HEAD_DIM = 128
LANES = 128
CONV_CARRY_ROWS = 8
VMEM_LIMIT_BYTES = 56 * 1024 * 1024


def _params(*semantics):
    return pltpu.CompilerParams(dimension_semantics=semantics, vmem_limit_bytes=VMEM_LIMIT_BYTES)


def _layer_norm(y, gain, bias):
    mu = jnp.mean(y, axis=-1, keepdims=True)
    yc = y - mu
    var = jnp.mean(yc * yc, axis=-1, keepdims=True)
    return yc * lax.rsqrt(var + LN_EPS) * gain + bias


def _ffn_ln_kernel(x_ref, wg_ref, wu_ref, wd_ref, gain_ref, bias_ref, o_ref, xb_sc, acc_sc, *, alpha):
    j = pl.program_id(1)

    @pl.when(j == 0)
    def _():
        xb_sc[...] = x_ref[...].astype(BF16)
        acc_sc[...] = jnp.zeros_like(acc_sc)

    xb = xb_sc[...]
    g = jnp.dot(xb, wg_ref[...], preferred_element_type=F32)
    u = jnp.dot(xb, wu_ref[...], preferred_element_type=F32)
    a = (g * jax.nn.sigmoid(g) * u).astype(BF16)
    acc_sc[...] += jnp.dot(a, wd_ref[...], preferred_element_type=F32)

    @pl.when(j == pl.num_programs(1) - 1)
    def _():
        y = alpha * x_ref[...].astype(F32) + 0.5 * acc_sc[...]
        o_ref[...] = _layer_norm(y, gain_ref[...], bias_ref[...]).astype(o_ref.dtype)


def _ffn_ln(x, wg, wu, wd, gain, bias, *, alpha, tm, tf, out_dtype):
    rows, d = x.shape
    dff = wg.shape[1]
    tm = min(tm, rows)
    return pl.pallas_call(
        functools.partial(_ffn_ln_kernel, alpha=alpha),
        grid=(rows // tm, dff // tf),
        in_specs=[
            pl.BlockSpec((tm, d), lambda i, j: (i, 0)),
            pl.BlockSpec((d, tf), lambda i, j: (0, j)),
            pl.BlockSpec((d, tf), lambda i, j: (0, j)),
            pl.BlockSpec((tf, d), lambda i, j: (j, 0)),
            pl.BlockSpec((1, d), lambda i, j: (0, 0)),
            pl.BlockSpec((1, d), lambda i, j: (0, 0)),
        ],
        out_specs=pl.BlockSpec((tm, d), lambda i, j: (i, 0)),
        out_shape=jax.ShapeDtypeStruct((rows, d), out_dtype),
        scratch_shapes=[pltpu.VMEM((tm, d), BF16), pltpu.VMEM((tm, d), F32)],
        compiler_params=_params("parallel", "arbitrary"),
        name="ffn_ln",
    )(x, wg, wu, wd, gain, bias)


def _conv_ln_kernel(x_ref, win_ref, cw_ref, wout_ref, gain_ref, bias_ref, carry_ref,
                    o_ref, tail_ref, ubuf, *, alpha, tiles_per_seq):
    i = pl.program_id(0)
    tm, d = x_ref.shape
    c = CONV_CARRY_ROWS

    @pl.when(i % tiles_per_seq == 0)
    def _():
        ubuf[0:c, :] = carry_ref[...]

    x = x_ref[...]
    proj = jnp.dot(x.astype(BF16), win_ref[...], preferred_element_type=F32)
    bgate = proj[:, 0:d]
    u = proj[:, d:2 * d] * proj[:, 2 * d:3 * d]
    ubuf[c:c + tm, :] = u
    cw = cw_ref[...]
    y = (cw[2:3, :] * u
         + cw[1:2, :] * ubuf[c - 1:c - 1 + tm, :]
         + cw[0:1, :] * ubuf[c - 2:c - 2 + tm, :])
    tail = ubuf[tm:tm + c, :]
    ubuf[0:c, :] = tail
    tail_ref[...] = tail
    mix = jnp.dot((bgate * y).astype(BF16), wout_ref[...], preferred_element_type=F32)
    o_ref[...] = _layer_norm(alpha * x + mix, gain_ref[...], bias_ref[...]).astype(o_ref.dtype)


def _conv_ln(x, w_in, conv_w, w_out, gain, bias, carry, *, alpha, tm, rows_per_seq):
    rows, d = x.shape
    tm = min(tm, rows_per_seq)
    const = lambda i: (0, 0)
    return pl.pallas_call(
        functools.partial(_conv_ln_kernel, alpha=alpha, tiles_per_seq=rows_per_seq // tm),
        grid=(rows // tm,),
        in_specs=[
            pl.BlockSpec((tm, d), lambda i: (i, 0)),
            pl.BlockSpec(w_in.shape, const),
            pl.BlockSpec(conv_w.shape, const),
            pl.BlockSpec(w_out.shape, const),
            pl.BlockSpec((1, d), const),
            pl.BlockSpec((1, d), const),
            pl.BlockSpec((CONV_CARRY_ROWS, d), const),
        ],
        out_specs=[pl.BlockSpec((tm, d), lambda i: (i, 0)),
                   pl.BlockSpec((CONV_CARRY_ROWS, d), const)],
        out_shape=[jax.ShapeDtypeStruct((rows, d), F32),
                   jax.ShapeDtypeStruct((CONV_CARRY_ROWS, d), F32)],
        scratch_shapes=[pltpu.VMEM((tm + CONV_CARRY_ROWS, d), F32)],
        compiler_params=_params("arbitrary"),
        name="conv_ln",
    )(x, w_in, conv_w, w_out, gain, bias, carry)


def _kv_kernel(x_ref, wkv_ref, wf_ref, fb_ref, cin_ref, k_ref, v_ref, ccol_ref, *rest,
               tiles_per_seq, emit_rows):
    if emit_rows:
        crow_ref, carry_sc = rest
    else:
        (carry_sc,) = rest
    i = pl.program_id(0)
    tm, d = x_ref.shape

    @pl.when(i % tiles_per_seq == 0)
    def _():
        carry_sc[...] = cin_ref[...]

    xb = x_ref[...].astype(BF16)
    kv = jnp.dot(xb, wkv_ref[...], preferred_element_type=F32)
    k_ref[...] = kv[:, 0:d].astype(BF16)
    v_ref[...] = kv[:, d:2 * d].astype(BF16)

    f = jnp.dot(xb, wf_ref[...], preferred_element_type=F32) + fb_ref[...]
    log_f = jnp.minimum(f, 0.0) - jnp.log1p(jnp.exp(-jnp.abs(f)))
    row = lax.broadcasted_iota(jnp.int32, log_f.shape, 0)
    csum = log_f
    shift = 1
    while shift < tm:
        csum = csum + jnp.where(row >= shift, pltpu.roll(csum, shift, axis=0), 0.0)
        shift *= 2
    csum = csum + carry_sc[...]
    carry_sc[...] = csum[tm - 1:tm, :]
    ccol_ref[...] = csum
    if emit_rows:
        crow_ref[...] = csum.T[0:crow_ref.shape[0], :]


def _shared_kv(x, w_kv, w_f, f_bias, c_in, *, tm, rows_per_seq, n_heads, emit_rows):
    rows, d = x.shape
    tm = min(tm, rows_per_seq)
    const = lambda i: (0, 0)
    out_specs = [pl.BlockSpec((tm, d), lambda i: (i, 0)),
                 pl.BlockSpec((tm, d), lambda i: (i, 0)),
                 pl.BlockSpec((tm, LANES), lambda i: (i, 0))]
    out_shape = [jax.ShapeDtypeStruct((rows, d), BF16),
                 jax.ShapeDtypeStruct((rows, d), BF16),
                 jax.ShapeDtypeStruct((rows, LANES), F32)]
    if emit_rows:
        out_specs.append(pl.BlockSpec((n_heads, tm), lambda i: (0, i)))
        out_shape.append(jax.ShapeDtypeStruct((n_heads, rows), F32))
    return pl.pallas_call(
        functools.partial(_kv_kernel, tiles_per_seq=rows_per_seq // tm, emit_rows=emit_rows),
        grid=(rows // tm,),
        in_specs=[
            pl.BlockSpec((tm, d), lambda i: (i, 0)),
            pl.BlockSpec(w_kv.shape, const),
            pl.BlockSpec(w_f.shape, const),
            pl.BlockSpec((1, LANES), const),
            pl.BlockSpec((1, LANES), const),
        ],
        out_specs=out_specs,
        out_shape=out_shape,
        scratch_shapes=[pltpu.VMEM((1, LANES), F32)],
        compiler_params=_params("arbitrary"),
        name="shared_kv",
    )(x, w_kv, w_f, f_bias, c_in)


def _qproj_kernel(x_ref, w_ref, q_ref, *, scale):
    q = jnp.dot(x_ref[...].astype(BF16), w_ref[...], preferred_element_type=F32)
    q_ref[...] = (q * scale).astype(q_ref.dtype)


def _qproj(x, w, *, scale, tm):
    rows, d = x.shape
    return pl.pallas_call(
        functools.partial(_qproj_kernel, scale=scale),
        grid=(rows // tm,),
        in_specs=[pl.BlockSpec((tm, d), lambda i: (i, 0)),
                  pl.BlockSpec(w.shape, lambda i: (0, 0))],
        out_specs=pl.BlockSpec((tm, w.shape[1]), lambda i: (i, 0)),
        out_shape=jax.ShapeDtypeStruct((rows, w.shape[1]), BF16),
        compiler_params=_params("parallel"),
        name="qproj",
    )(x, w)


def _attn_kernel(q_ref, k_ref, v_ref, ccol_ref, crow_ref, km_ref, vm_ref, cm_ref, o_ref, *, tk):
    h = pl.program_id(1)
    qi = pl.program_id(2)
    tq = q_ref.shape[0]
    nt = (((1,), (1,)), ((), ()))

    q = q_ref[...]
    lane = lax.broadcasted_iota(jnp.int32, ccol_ref.shape, 1)
    cq = jnp.sum(jnp.where(lane == h, ccol_ref[...], 0.0), axis=1, keepdims=True)

    s = lax.dot_general(q, km_ref[...], nt, preferred_element_type=F32) + cq - cm_ref[pl.ds(h, 1), :]
    m = jnp.max(s, axis=1, keepdims=True)
    p = jnp.exp(s - m)
    l = jnp.sum(p, axis=1, keepdims=True)
    acc = jnp.dot(p.astype(BF16), vm_ref[...], preferred_element_type=F32)

    def scores(j):
        off = pl.multiple_of(j * tk, tk)
        kblk = k_ref[pl.ds(off, tk), :]
        ck = crow_ref[h, pl.ds(j, 1), :]
        return lax.dot_general(q, kblk, nt, preferred_element_type=F32) + cq - ck, off

    def update(s, off, carry):
        m, l, acc = carry
        m_new = jnp.maximum(m, jnp.max(s, axis=1, keepdims=True))
        corr = jnp.exp(m - m_new)
        p = jnp.exp(s - m_new)
        l = corr * l + jnp.sum(p, axis=1, keepdims=True)
        pv = jnp.dot(p.astype(BF16), v_ref[pl.ds(off, tk), :], preferred_element_type=F32)
        return m_new, l, corr * acc + pv

    def full_block(j, carry):
        s, off = scores(j)
        return update(s, off, carry)

    carry = lax.fori_loop(0, qi, full_block, (m, l, acc))

    s, off = scores(qi)
    r = lax.broadcasted_iota(jnp.int32, s.shape, 0)
    cidx = lax.broadcasted_iota(jnp.int32, s.shape, 1)
    s = jnp.where(cidx <= r, s, -MASK_VALUE)
    m, l, acc = update(s, off, carry)
    o_ref[...] = (acc / l).astype(o_ref.dtype)


def _attention(q, k, v, ccol, crow, km, vm, cm, *, batch, seq, n_heads, tq):
    rows, d = q.shape
    nq = seq // tq
    tk = tq
    crow3 = crow.reshape(n_heads, batch * nq, tk)
    return pl.pallas_call(
        functools.partial(_attn_kernel, tk=tk),
        grid=(batch, n_heads, nq),
        in_specs=[
            pl.BlockSpec((tq, HEAD_DIM), lambda b, h, i: (b * nq + i, h)),
            pl.BlockSpec((seq, HEAD_DIM), lambda b, h, i: (b, h)),
            pl.BlockSpec((seq, HEAD_DIM), lambda b, h, i: (b, h)),
            pl.BlockSpec((tq, LANES), lambda b, h, i: (b * nq + i, 0)),
            pl.BlockSpec((n_heads, nq, tk), lambda b, h, i: (0, b, 0)),
            pl.BlockSpec((LANES, HEAD_DIM), lambda b, h, i: (0, h)),
            pl.BlockSpec((LANES, HEAD_DIM), lambda b, h, i: (0, h)),
            pl.BlockSpec((n_heads, LANES), lambda b, h, i: (0, 0)),
        ],
        out_specs=pl.BlockSpec((tq, HEAD_DIM), lambda b, h, i: (b * nq + i, h)),
        out_shape=jax.ShapeDtypeStruct((rows, d), BF16),
        compiler_params=_params("parallel", "parallel", "arbitrary"),
        name="fox_attention",
    )(q, k, v, ccol, crow3, km, vm, cm)


def _oproj_ln_kernel(o_ref, x_ref, w_ref, gain_ref, bias_ref, y_ref, *, alpha):
    mix = jnp.dot(o_ref[...], w_ref[...], preferred_element_type=F32)
    y_ref[...] = _layer_norm(alpha * x_ref[...] + mix, gain_ref[...], bias_ref[...]).astype(y_ref.dtype)


def _oproj_ln(o, x, w, gain, bias, *, alpha, tm):
    rows, d = x.shape
    const = lambda i: (0, 0)
    return pl.pallas_call(
        functools.partial(_oproj_ln_kernel, alpha=alpha),
        grid=(rows // tm,),
        in_specs=[pl.BlockSpec((tm, d), lambda i: (i, 0)),
                  pl.BlockSpec((tm, d), lambda i: (i, 0)),
                  pl.BlockSpec(w.shape, const),
                  pl.BlockSpec((1, d), const),
                  pl.BlockSpec((1, d), const)],
        out_specs=pl.BlockSpec((tm, d), lambda i: (i, 0)),
        out_shape=jax.ShapeDtypeStruct((rows, d), F32),
        compiler_params=_params("parallel"),
        name="oproj_ln",
    )(o, x, w, gain, bias)


def kernel(x, meta, ffn1_wg, ffn1_wu, ffn1_wd, ffn2_wg, ffn2_wu, ffn2_wd, ln_gain, ln_bias,
           conv_w_in, conv_w, conv_w_out, kv_w, f_bias, attn_w_q, attn_w_o):
    batch, seq, d = x.shape
    n_meta = meta.shape[0]
    depth = ffn1_wg.shape[0]
    n_heads = f_bias.shape[0]
    assert depth == 2 and conv_w_in.shape[0] == 1 and attn_w_q.shape[0] == 1
    assert d == n_heads * HEAD_DIM and n_meta <= LANES and n_meta % 16 == 0
    alpha = (2 * depth) ** 0.25
    scale = 1.0 / math.sqrt(HEAD_DIM)

    bf = lambda w: w.astype(BF16)
    gain = lambda l, i: ln_gain[l, i].reshape(1, d).astype(F32)
    bias = lambda l, i: ln_bias[l, i].reshape(1, d).astype(F32)
    ffn_w = {(1, l): (bf(ffn1_wg[l]), bf(ffn1_wu[l]), bf(ffn1_wd[l])) for l in range(depth)}
    ffn_w.update({(2, l): (bf(ffn2_wg[l]), bf(ffn2_wu[l]), bf(ffn2_wd[l])) for l in range(depth)})

    def ffn(h, which, l, out_dtype=F32):
        wg, wu, wd = ffn_w[(which, l)]
        return _ffn_ln(h, wg, wu, wd, gain(l, 0 if which == 1 else 2), bias(l, 0 if which == 1 else 2),
                       alpha=alpha, tm=1024, tf=256, out_dtype=out_dtype)

    xs = x.reshape(batch * seq, d)
    ms = meta.astype(x.dtype)

    w_in, w_out, cw = bf(conv_w_in[0]), bf(conv_w_out[0]), conv_w[0].astype(F32)
    conv = functools.partial(_conv_ln, alpha=alpha, tm=512)
    m1 = ffn(ms, 1, 0)
    h1 = ffn(xs, 1, 0)
    zero_carry = jnp.zeros((CONV_CARRY_ROWS, d), F32)
    m2, m_tail = conv(m1, w_in, cw, w_out, gain(0, 1), bias(0, 1), zero_carry, rows_per_seq=n_meta)
    h2, _ = conv(h1, w_in, cw, w_out, gain(0, 1), bias(0, 1), m_tail, rows_per_seq=seq)
    m3 = ffn(m2, 2, 0)
    h3 = ffn(h2, 2, 0)

    w_kv = bf(kv_w[:, :2 * d])
    w_f = jnp.pad(bf(kv_w[:, 2 * d:]), ((0, 0), (0, LANES - n_heads)))
    fb = jnp.pad(f_bias.astype(F32), (0, LANES - n_heads)).reshape(1, LANES)
    kv = functools.partial(_shared_kv, tm=512, n_heads=n_heads)
    km, vm, cm_col = kv(m3, w_kv, w_f, fb, jnp.zeros((1, LANES), F32), rows_per_seq=n_meta, emit_rows=False)
    k, v, ccol, crow = kv(h3, w_kv, w_f, fb, cm_col[n_meta - 1:n_meta], rows_per_seq=seq, emit_rows=True)
    km = jnp.pad(km, ((0, LANES - n_meta), (0, 0)))
    vm = jnp.pad(vm, ((0, LANES - n_meta), (0, 0)))
    cm = jnp.pad(cm_col[:, :n_heads].T, ((0, 0), (0, LANES - n_meta)), constant_values=MASK_VALUE)

    h4 = ffn(h3, 1, 1)
    q = _qproj(h4, bf(attn_w_q[0]), scale=scale, tm=1024)
    o = _attention(q, k, v, ccol, crow, km, vm, cm, batch=batch, seq=seq, n_heads=n_heads, tq=512)
    h5 = _oproj_ln(o, h4, bf(attn_w_o[0]), gain(1, 1), bias(1, 1), alpha=alpha, tm=1024)
    out = ffn(h5, 2, 1, out_dtype=x.dtype)
    return out.reshape(batch, seq, d)
```

```python
import functools
import math

import jax
import jax.numpy as jnp
import numpy as np
from jax import lax
from jax.experimental import pallas as pl
from jax.experimental.pallas import tpu as pltpu

F32 = jnp.float32
BF16 = jnp.bfloat16

LN_EPS = 1e-5
MASK_VALUE = 1e30
LOG2E = math.log2(math.e)
HEAD_DIM = 128
LANES = 128
CONV_CARRY_ROWS = 8
VMEM_LIMIT_BYTES = 56 * 1024 * 1024

N_SPLIT = 3
HEAD_GROUP = 8
AUX_Q_LANE = 0
AUX_K_LANE = N_SPLIT * HEAD_GROUP
QK_LOOKAHEAD = 2
SCORE_SLOTS = QK_LOOKAHEAD + 2
KV_UNROLL = 2
BF16_SUBLANES = 16
V_ROWS = HEAD_DIM + BF16_SUBLANES


def _ones_row_tile(width):
    row = lax.broadcasted_iota(jnp.int32, (BF16_SUBLANES, width), 0)
    return jnp.where(row == 0, 1.0, 0.0).astype(BF16)


def _params(*semantics):
    return pltpu.CompilerParams(dimension_semantics=semantics, vmem_limit_bytes=VMEM_LIMIT_BYTES)


def _layer_norm(y, gain, bias):
    mu = jnp.mean(y, axis=-1, keepdims=True)
    yc = y - mu
    var = jnp.mean(yc * yc, axis=-1, keepdims=True)
    return yc * lax.rsqrt(var + LN_EPS) * gain + bias


def _ffn_ln_kernel(x_ref, wg_ref, wu_ref, wd_ref, gain_ref, bias_ref, o_ref, xb_sc, acc_sc, *, alpha):
    j = pl.program_id(1)

    @pl.when(j == 0)
    def _():
        xb_sc[...] = x_ref[...].astype(BF16)
        acc_sc[...] = jnp.zeros_like(acc_sc)

    xb = xb_sc[...]
    g = jnp.dot(xb, wg_ref[...], preferred_element_type=F32)
    u = jnp.dot(xb, wu_ref[...], preferred_element_type=F32)
    a = (g * jax.nn.sigmoid(g) * u).astype(BF16)
    acc_sc[...] += jnp.dot(a, wd_ref[...], preferred_element_type=F32)

    @pl.when(j == pl.num_programs(1) - 1)
    def _():
        y = alpha * x_ref[...].astype(F32) + 0.5 * acc_sc[...]
        o_ref[...] = _layer_norm(y, gain_ref[...], bias_ref[...]).astype(o_ref.dtype)


def _ffn_ln(x, wg, wu, wd, gain, bias, *, alpha, tm, tf, out_dtype):
    rows, d = x.shape
    dff = wg.shape[1]
    tm = min(tm, rows)
    return pl.pallas_call(
        functools.partial(_ffn_ln_kernel, alpha=alpha),
        grid=(rows // tm, dff // tf),
        in_specs=[
            pl.BlockSpec((tm, d), lambda i, j: (i, 0)),
            pl.BlockSpec((d, tf), lambda i, j: (0, j)),
            pl.BlockSpec((d, tf), lambda i, j: (0, j)),
            pl.BlockSpec((tf, d), lambda i, j: (j, 0)),
            pl.BlockSpec((1, d), lambda i, j: (0, 0)),
            pl.BlockSpec((1, d), lambda i, j: (0, 0)),
        ],
        out_specs=pl.BlockSpec((tm, d), lambda i, j: (i, 0)),
        out_shape=jax.ShapeDtypeStruct((rows, d), out_dtype),
        scratch_shapes=[pltpu.VMEM((tm, d), BF16), pltpu.VMEM((tm, d), F32)],
        compiler_params=_params("parallel", "arbitrary"),
        name="ffn_ln",
    )(x, wg, wu, wd, gain, bias)


def _conv_ln_kernel(x_ref, win_ref, cw_ref, wout_ref, gain_ref, bias_ref, carry_ref,
                    o_ref, tail_ref, ubuf, *, alpha, tiles_per_seq):
    i = pl.program_id(0)
    tm, d = x_ref.shape
    c = CONV_CARRY_ROWS

    @pl.when(i % tiles_per_seq == 0)
    def _():
        ubuf[0:c, :] = carry_ref[...]

    x = x_ref[...]
    proj = jnp.dot(x.astype(BF16), win_ref[...], preferred_element_type=F32)
    bgate = proj[:, 0:d]
    u = proj[:, d:2 * d] * proj[:, 2 * d:3 * d]
    ubuf[c:c + tm, :] = u
    cw = cw_ref[...]
    y = (cw[2:3, :] * u
         + cw[1:2, :] * ubuf[c - 1:c - 1 + tm, :]
         + cw[0:1, :] * ubuf[c - 2:c - 2 + tm, :])
    tail = ubuf[tm:tm + c, :]
    ubuf[0:c, :] = tail
    tail_ref[...] = tail
    mix = jnp.dot((bgate * y).astype(BF16), wout_ref[...], preferred_element_type=F32)
    o_ref[...] = _layer_norm(alpha * x + mix, gain_ref[...], bias_ref[...]).astype(o_ref.dtype)


def _conv_ln(x, w_in, conv_w, w_out, gain, bias, carry, *, alpha, tm, rows_per_seq):
    rows, d = x.shape
    tm = min(tm, rows_per_seq)
    const = lambda i: (0, 0)
    return pl.pallas_call(
        functools.partial(_conv_ln_kernel, alpha=alpha, tiles_per_seq=rows_per_seq // tm),
        grid=(rows // tm,),
        in_specs=[
            pl.BlockSpec((tm, d), lambda i: (i, 0)),
            pl.BlockSpec(w_in.shape, const),
            pl.BlockSpec(conv_w.shape, const),
            pl.BlockSpec(w_out.shape, const),
            pl.BlockSpec((1, d), const),
            pl.BlockSpec((1, d), const),
            pl.BlockSpec((CONV_CARRY_ROWS, d), const),
        ],
        out_specs=[pl.BlockSpec((tm, d), lambda i: (i, 0)),
                   pl.BlockSpec((CONV_CARRY_ROWS, d), const)],
        out_shape=[jax.ShapeDtypeStruct((rows, d), F32),
                   jax.ShapeDtypeStruct((CONV_CARRY_ROWS, d), F32)],
        scratch_shapes=[pltpu.VMEM((tm + CONV_CARRY_ROWS, d), F32)],
        compiler_params=_params("arbitrary"),
        name="conv_ln",
    )(x, w_in, conv_w, w_out, gain, bias, carry)


def _kv_kernel(x_ref, wkv_ref, wf_ref, fb_ref, cin_ref, k_ref, v_ref, c_ref, auxq_ref, auxk_ref,
               carry_sc, *, tiles_per_seq, n_heads, transpose_v):
    i = pl.program_id(0)
    tm, d = x_ref.shape

    @pl.when(i % tiles_per_seq == 0)
    def _():
        carry_sc[...] = cin_ref[...]

    xb = x_ref[...].astype(BF16)
    kv = jnp.dot(xb, wkv_ref[...], preferred_element_type=F32)
    k_ref[...] = kv[:, 0:d].astype(BF16)
    if transpose_v:
        vt = kv[:, d:2 * d].T.astype(BF16)
        for hh in range(n_heads):
            v_ref[0, hh * V_ROWS:hh * V_ROWS + HEAD_DIM, :] = vt[hh * HEAD_DIM:(hh + 1) * HEAD_DIM, :]
            v_ref[0, hh * V_ROWS + HEAD_DIM:(hh + 1) * V_ROWS, :] = _ones_row_tile(tm)
    else:
        v_ref[...] = kv[:, d:2 * d].astype(BF16)

    f = jnp.dot(xb, wf_ref[...], preferred_element_type=F32) + fb_ref[...]
    log_f = jnp.minimum(f, 0.0) - jnp.log1p(jnp.exp(-jnp.abs(f)))
    row = lax.broadcasted_iota(jnp.int32, log_f.shape, 0)
    csum = log_f
    shift = 1
    while shift < tm:
        csum = csum + jnp.where(row >= shift, pltpu.roll(csum, shift, axis=0), 0.0)
        shift *= 2
    csum = csum + carry_sc[...]
    carry_sc[...] = csum[tm - 1:tm, :]
    c_ref[...] = csum

    lane = lax.broadcasted_iota(jnp.int32, csum.shape, 1)
    c2 = jnp.where(lane < n_heads, csum * LOG2E, 0.0)
    hi = c2.astype(BF16).astype(F32)
    r1 = c2 - hi
    mid = r1.astype(BF16).astype(F32)
    lo = r1 - mid
    aux = hi + pltpu.roll(mid, HEAD_GROUP, axis=1) + pltpu.roll(lo, 2 * HEAD_GROUP, axis=1)
    auxq_ref[...] = aux.astype(BF16)
    auxk_ref[...] = (-pltpu.roll(aux, AUX_K_LANE - AUX_Q_LANE, axis=1)).astype(BF16)


def _shared_kv(x, w_kv, w_f, f_bias, c_in, *, tm, rows_per_seq, n_heads, transpose_v):
    rows, d = x.shape
    tm = min(tm, rows_per_seq)
    const = lambda i: (0, 0)
    row_tile = lambda i: (i, 0)
    if transpose_v:
        v_spec = pl.BlockSpec((1, n_heads * V_ROWS, tm), lambda i: (i, 0, 0))
        v_shape = jax.ShapeDtypeStruct((rows // tm, n_heads * V_ROWS, tm), BF16)
    else:
        v_spec = pl.BlockSpec((tm, d), row_tile)
        v_shape = jax.ShapeDtypeStruct((rows, d), BF16)
    return pl.pallas_call(
        functools.partial(_kv_kernel, tiles_per_seq=rows_per_seq // tm, n_heads=n_heads,
                          transpose_v=transpose_v),
        grid=(rows // tm,),
        in_specs=[
            pl.BlockSpec((tm, d), row_tile),
            pl.BlockSpec(w_kv.shape, const),
            pl.BlockSpec(w_f.shape, const),
            pl.BlockSpec((1, LANES), const),
            pl.BlockSpec((1, LANES), const),
        ],
        out_specs=[pl.BlockSpec((tm, d), row_tile), v_spec,
                   pl.BlockSpec((tm, LANES), row_tile),
                   pl.BlockSpec((tm, LANES), row_tile),
                   pl.BlockSpec((tm, LANES), row_tile)],
        out_shape=[jax.ShapeDtypeStruct((rows, d), BF16), v_shape,
                   jax.ShapeDtypeStruct((rows, LANES), F32),
                   jax.ShapeDtypeStruct((rows, LANES), BF16),
                   jax.ShapeDtypeStruct((rows, LANES), BF16)],
        scratch_shapes=[pltpu.VMEM((1, LANES), F32)],
        compiler_params=_params("arbitrary"),
        name="shared_kv",
    )(x, w_kv, w_f, f_bias, c_in)


def _qproj_kernel(x_ref, w_ref, q_ref, *, scale):
    q = jnp.dot(x_ref[...].astype(BF16), w_ref[...], preferred_element_type=F32)
    q_ref[...] = (q * scale).astype(q_ref.dtype)


def _qproj(x, w, *, scale, tm):
    rows, d = x.shape
    return pl.pallas_call(
        functools.partial(_qproj_kernel, scale=scale),
        grid=(rows // tm,),
        in_specs=[pl.BlockSpec((tm, d), lambda i: (i, 0)),
                  pl.BlockSpec(w.shape, lambda i: (0, 0))],
        out_specs=pl.BlockSpec((tm, w.shape[1]), lambda i: (i, 0)),
        out_shape=jax.ShapeDtypeStruct((rows, w.shape[1]), BF16),
        compiler_params=_params("parallel"),
        name="qproj",
    )(x, w)


def _head_one_hot(h, base):
    lane = lax.broadcasted_iota(jnp.int32, (1, LANES), 1)
    hit = (lane == base + h) | (lane == base + HEAD_GROUP + h) | (lane == base + 2 * HEAD_GROUP + h)
    return jnp.where(hit, 1.0, 0.0).astype(BF16)


def _attn_kernel(q_ref, auxq_ref, k_ref, auxk_ref, vt_ref, km_ref, auxkm_ref, vtm_ref, o_ref,
                 qa_sc, s_sc, m_sc, acc_sc, *, tk):
    h = pl.program_id(1)
    qi = pl.program_id(2)
    tq = q_ref.shape[0]
    ts = tk
    n_strips = tq // ts
    nt = (((1,), (1,)), ((), ()))

    q_sel = _head_one_hot(h, AUX_K_LANE)
    k_sel = _head_one_hot(h, AUX_Q_LANE)
    qa_sc[:, 0:HEAD_DIM] = q_ref[...]
    qa_sc[:, HEAD_DIM:] = auxq_ref[...] + q_sel
    m_sc[...] = jnp.full(m_sc.shape, -MASK_VALUE, F32)
    acc_sc[...] = jnp.zeros(acc_sc.shape, F32)

    def strip_scores(t, ka, c, causal):
        cs = slice(c * ts, (c + 1) * ts)
        s = lax.dot_general(ka, qa_sc[cs, :], nt, preferred_element_type=F32)
        if causal:
            key = lax.broadcasted_iota(jnp.int32, s.shape, 0)
            qry = lax.broadcasted_iota(jnp.int32, s.shape, 1)
            s = jnp.where(key <= qry, s, -MASK_VALUE)
        s_sc[t % SCORE_SLOTS, 0:s.shape[0], :] = s
        return jnp.max(s, axis=0, keepdims=True)

    def strip_update(t, s_max, vt, c):
        cs = slice(c * ts, (c + 1) * ts)
        m = m_sc[:, cs]
        m_new = jnp.maximum(m, s_max)
        corr = jnp.exp2(m - m_new)
        m_sc[:, cs] = m_new
        p = jnp.exp2(s_sc[t % SCORE_SLOTS, 0:vt.shape[1], :] - m_new)
        pv = jnp.dot(vt, p.astype(BF16), preferred_element_type=F32)
        acc_sc[:, cs] = corr * acc_sc[:, cs] + pv

    def run_tasks(tasks):
        pending = [strip_scores(t, ka, c, causal)
                   for t, (ka, _, c, causal) in enumerate(tasks[:QK_LOOKAHEAD])]
        for t, (_, vt, c, _) in enumerate(tasks):
            if t + QK_LOOKAHEAD < len(tasks):
                ka_n, _, c_n, causal_n = tasks[t + QK_LOOKAHEAD]
                pending.append(strip_scores(t + QK_LOOKAHEAD, ka_n, c_n, causal_n))
            strip_update(t, pending.pop(0), vt, c)

    def key_block(j):
        off = pl.multiple_of(j * tk, tk)
        ka = jnp.concatenate([k_ref[pl.ds(off, tk), :], auxk_ref[pl.ds(off, tk), :] + k_sel], axis=1)
        return ka, vt_ref[j]

    ka_m = jnp.concatenate([km_ref[...], auxkm_ref[...] + k_sel], axis=1)
    run_tasks([(ka_m, vtm_ref[...], c, False) for c in range(n_strips)])

    def full_blocks(jj, _):
        blocks = [key_block(jj * KV_UNROLL + u) for u in range(KV_UNROLL)]
        run_tasks([(ka, vt, c, False) for ka, vt in blocks for c in range(n_strips)])
        return 0

    assert n_strips % KV_UNROLL == 0
    lax.fori_loop(0, qi * (n_strips // KV_UNROLL), full_blocks, 0)

    diag = [key_block(qi * n_strips + d) for d in range(n_strips)]
    run_tasks([(diag[d][0], diag[d][1], c, c == d) for d in range(n_strips) for c in range(d, n_strips)])

    denom = acc_sc[HEAD_DIM:HEAD_DIM + 1, :]
    o_ref[...] = (acc_sc[0:HEAD_DIM, :] / denom).T.astype(o_ref.dtype)


def _attention(q, auxq, k, auxk, vt, km, auxkm, vtm, *, batch, seq, n_heads, tq, tk):
    rows, d = q.shape
    nq = seq // tq
    nk = seq // tk
    assert vt.shape == (batch * nk, n_heads * V_ROWS, tk)
    return pl.pallas_call(
        functools.partial(_attn_kernel, tk=tk),
        grid=(batch, n_heads, nq),
        in_specs=[
            pl.BlockSpec((tq, HEAD_DIM), lambda b, h, i: (b * nq + i, h)),
            pl.BlockSpec((tq, LANES), lambda b, h, i: (b * nq + i, 0)),
            pl.BlockSpec((seq, HEAD_DIM), lambda b, h, i: (b, h)),
            pl.BlockSpec((seq, LANES), lambda b, h, i: (b, 0)),
            pl.BlockSpec((nk, V_ROWS, tk), lambda b, h, i: (b, h, 0)),
            pl.BlockSpec((LANES, HEAD_DIM), lambda b, h, i: (0, h)),
            pl.BlockSpec((LANES, LANES), lambda b, h, i: (0, 0)),
            pl.BlockSpec((V_ROWS, LANES), lambda b, h, i: (h, 0)),
        ],
        out_specs=pl.BlockSpec((tq, HEAD_DIM), lambda b, h, i: (b * nq + i, h)),
        out_shape=jax.ShapeDtypeStruct((rows, d), BF16),
        scratch_shapes=[pltpu.VMEM((tq, HEAD_DIM + LANES), BF16),
                        pltpu.VMEM((SCORE_SLOTS, tk, tk), F32),
                        pltpu.VMEM((1, tq), F32),
                        pltpu.VMEM((V_ROWS, tq), F32)],
        compiler_params=_params("parallel", "parallel", "arbitrary"),
        name="fox_attention",
    )(q, auxq, k, auxk, vt, km, auxkm, vtm)


def _oproj_ln_kernel(o_ref, x_ref, w_ref, gain_ref, bias_ref, y_ref, *, alpha):
    mix = jnp.dot(o_ref[...], w_ref[...], preferred_element_type=F32)
    y_ref[...] = _layer_norm(alpha * x_ref[...] + mix, gain_ref[...], bias_ref[...]).astype(y_ref.dtype)


def _oproj_ln(o, x, w, gain, bias, *, alpha, tm):
    rows, d = x.shape
    const = lambda i: (0, 0)
    return pl.pallas_call(
        functools.partial(_oproj_ln_kernel, alpha=alpha),
        grid=(rows // tm,),
        in_specs=[pl.BlockSpec((tm, d), lambda i: (i, 0)),
                  pl.BlockSpec((tm, d), lambda i: (i, 0)),
                  pl.BlockSpec(w.shape, const),
                  pl.BlockSpec((1, d), const),
                  pl.BlockSpec((1, d), const)],
        out_specs=pl.BlockSpec((tm, d), lambda i: (i, 0)),
        out_shape=jax.ShapeDtypeStruct((rows, d), F32),
        compiler_params=_params("parallel"),
        name="oproj_ln",
    )(o, x, w, gain, bias)


def kernel(x, meta, ffn1_wg, ffn1_wu, ffn1_wd, ffn2_wg, ffn2_wu, ffn2_wd, ln_gain, ln_bias,
           conv_w_in, conv_w, conv_w_out, kv_w, f_bias, attn_w_q, attn_w_o):
    batch, seq, d = x.shape
    n_meta = meta.shape[0]
    depth = ffn1_wg.shape[0]
    n_heads = f_bias.shape[0]
    assert depth == 2 and conv_w_in.shape[0] == 1 and attn_w_q.shape[0] == 1
    assert d == n_heads * HEAD_DIM and n_heads == HEAD_GROUP
    assert n_meta <= LANES and n_meta % 16 == 0
    alpha = (2 * depth) ** 0.25
    q_scale = LOG2E / math.sqrt(HEAD_DIM)
    attn_tile = 512

    bf = lambda w: w.astype(BF16)
    gain = lambda l, i: ln_gain[l, i].reshape(1, d).astype(F32)
    bias = lambda l, i: ln_bias[l, i].reshape(1, d).astype(F32)
    ffn_w = {(1, l): (bf(ffn1_wg[l]), bf(ffn1_wu[l]), bf(ffn1_wd[l])) for l in range(depth)}
    ffn_w.update({(2, l): (bf(ffn2_wg[l]), bf(ffn2_wu[l]), bf(ffn2_wd[l])) for l in range(depth)})

    def ffn(h, which, l, out_dtype=F32):
        wg, wu, wd = ffn_w[(which, l)]
        return _ffn_ln(h, wg, wu, wd, gain(l, 0 if which == 1 else 2), bias(l, 0 if which == 1 else 2),
                       alpha=alpha, tm=1024, tf=256, out_dtype=out_dtype)

    xs = x.reshape(batch * seq, d)
    ms = meta.astype(x.dtype)

    w_in, w_out, cw = bf(conv_w_in[0]), bf(conv_w_out[0]), conv_w[0].astype(F32)
    conv = functools.partial(_conv_ln, alpha=alpha, tm=512)
    m1 = ffn(ms, 1, 0)
    h1 = ffn(xs, 1, 0)
    zero_carry = jnp.zeros((CONV_CARRY_ROWS, d), F32)
    m2, m_tail = conv(m1, w_in, cw, w_out, gain(0, 1), bias(0, 1), zero_carry, rows_per_seq=n_meta)
    h2, _ = conv(h1, w_in, cw, w_out, gain(0, 1), bias(0, 1), m_tail, rows_per_seq=seq)
    m3 = ffn(m2, 2, 0)
    h3 = ffn(h2, 2, 0)

    w_kv = bf(kv_w[:, :2 * d])
    w_f = jnp.pad(bf(kv_w[:, 2 * d:]), ((0, 0), (0, LANES - n_heads)))
    fb = jnp.pad(f_bias.astype(F32), (0, LANES - n_heads)).reshape(1, LANES)
    kv = functools.partial(_shared_kv, tm=attn_tile, n_heads=n_heads)
    km, vm, cm, _, auxkm = kv(m3, w_kv, w_f, fb, jnp.zeros((1, LANES), F32),
                              rows_per_seq=n_meta, transpose_v=False)
    k, vt, _, auxq, auxk = kv(h3, w_kv, w_f, fb, cm[n_meta - 1:n_meta],
                              rows_per_seq=seq, transpose_v=True)
    pad_aux = np.zeros((LANES - n_meta, LANES), np.float32)
    pad_aux[:, AUX_K_LANE:AUX_K_LANE + n_heads] = -MASK_VALUE
    km = jnp.pad(km, ((0, LANES - n_meta), (0, 0)))
    auxkm = jnp.concatenate([auxkm, jnp.asarray(pad_aux, BF16)], axis=0)
    ones_tile = np.zeros((n_heads, BF16_SUBLANES, LANES), np.float32)
    ones_tile[:, 0, :] = 1.0
    vtm = jnp.pad(vm.T, ((0, 0), (0, LANES - n_meta))).reshape(n_heads, HEAD_DIM, LANES)
    vtm = jnp.concatenate([vtm, jnp.asarray(ones_tile, BF16)], axis=1).reshape(n_heads * V_ROWS, LANES)

    h4 = ffn(h3, 1, 1)
    q = _qproj(h4, bf(attn_w_q[0]), scale=q_scale, tm=1024)
    o = _attention(q, auxq, k, auxk, vt, km, auxkm, vtm, batch=batch, seq=seq, n_heads=n_heads,
                   tq=4 * attn_tile, tk=attn_tile)
    h5 = _oproj_ln(o, h4, bf(attn_w_o[0]), gain(1, 1), bias(1, 1), alpha=alpha, tm=1024)
    out = ffn(h5, 2, 1, out_dtype=x.dtype)
    return out.reshape(batch, seq, d)
```

```python
import functools
import math

import jax
import jax.numpy as jnp
import numpy as np
from jax import lax
from jax.experimental import pallas as pl
from jax.experimental.pallas import tpu as pltpu

F32 = jnp.float32
BF16 = jnp.bfloat16

LN_EPS = 1e-5
MASK_VALUE = 1e30
LOG2E = math.log2(math.e)
HEAD_DIM = 128
LANES = 128
CONV_CARRY_ROWS = 8
FFN_CHUNK = 256
VMEM_LIMIT_BYTES = 56 * 1024 * 1024

N_SPLIT = 3
HEAD_GROUP = 8
AUX_Q_LANE = 0
AUX_K_LANE = N_SPLIT * HEAD_GROUP
QK_LOOKAHEAD = 2
SCORE_SLOTS = QK_LOOKAHEAD + 2
KV_UNROLL = 2
BF16_SUBLANES = 16
V_ROWS = HEAD_DIM + BF16_SUBLANES


def _ones_row_tile(width):
    row = lax.broadcasted_iota(jnp.int32, (BF16_SUBLANES, width), 0)
    return jnp.where(row == 0, 1.0, 0.0).astype(BF16)


def _params(*semantics):
    return pltpu.CompilerParams(dimension_semantics=semantics, vmem_limit_bytes=VMEM_LIMIT_BYTES)


def _layer_norm(y, gain, bias):
    mu = jnp.mean(y, axis=-1, keepdims=True)
    yc = y - mu
    var = jnp.mean(yc * yc, axis=-1, keepdims=True)
    return yc * lax.rsqrt(var + LN_EPS) * gain + bias


def _ffn_ln_kernel(x_ref, wgu_ref, wd_ref, gain_ref, bias_ref, o_ref, xb_sc, gu_sc, a_sc, *, alpha):
    n_chunks, _, tf2 = wgu_ref.shape
    tf = tf2 // 2
    xb_sc[...] = x_ref[...].astype(BF16)

    def gate_up(c):
        gu_sc[c % 2] = jnp.dot(xb_sc[...], wgu_ref[c], preferred_element_type=F32)

    gate_up(0)
    for c in range(n_chunks):
        if c + 1 < n_chunks:
            gate_up(c + 1)
        g = gu_sc[c % 2, :, 0:tf]
        u = gu_sc[c % 2, :, tf:tf2]
        a_sc[:, c * tf:(c + 1) * tf] = (g * jax.nn.sigmoid(g) * u).astype(BF16)
    half_ffn = jnp.dot(a_sc[...], wd_ref[...], preferred_element_type=F32)
    y = alpha * x_ref[...].astype(F32) + half_ffn
    o_ref[...] = _layer_norm(y, gain_ref[...], bias_ref[...]).astype(o_ref.dtype)


def _resident(shape):
    return pl.BlockSpec(shape, lambda i: (0,) * len(shape), pipeline_mode=pl.Buffered(1))


def _ffn_ln(x, wgu, wd, gain, bias, *, alpha, tm, out_dtype):
    rows, d = x.shape
    n_chunks, _, tf2 = wgu.shape
    dff = wd.shape[0]
    tm = min(tm, rows)
    return pl.pallas_call(
        functools.partial(_ffn_ln_kernel, alpha=alpha),
        grid=(rows // tm,),
        in_specs=[
            pl.BlockSpec((tm, d), lambda i: (i, 0)),
            _resident(wgu.shape),
            _resident(wd.shape),
            _resident((1, d)),
            _resident((1, d)),
        ],
        out_specs=pl.BlockSpec((tm, d), lambda i: (i, 0)),
        out_shape=jax.ShapeDtypeStruct((rows, d), out_dtype),
        scratch_shapes=[pltpu.VMEM((tm, d), BF16), pltpu.VMEM((2, tm, tf2), F32),
                        pltpu.VMEM((tm, dff), BF16)],
        compiler_params=_params("parallel"),
        name="ffn_ln",
    )(x, wgu, wd, gain, bias)


def _conv_ln_kernel(x_ref, win_ref, cw_ref, wout_ref, gain_ref, bias_ref, carry_ref,
                    o_ref, tail_ref, ubuf, *, alpha, tiles_per_seq):
    i = pl.program_id(0)
    tm, d = x_ref.shape
    c = CONV_CARRY_ROWS

    @pl.when(i % tiles_per_seq == 0)
    def _():
        ubuf[0:c, :] = carry_ref[...]

    x = x_ref[...]
    proj = jnp.dot(x.astype(BF16), win_ref[...], preferred_element_type=F32)
    bgate = proj[:, 0:d]
    u = proj[:, d:2 * d] * proj[:, 2 * d:3 * d]
    ubuf[c:c + tm, :] = u
    cw = cw_ref[...]
    y = (cw[2:3, :] * u
         + cw[1:2, :] * ubuf[c - 1:c - 1 + tm, :]
         + cw[0:1, :] * ubuf[c - 2:c - 2 + tm, :])
    tail = ubuf[tm:tm + c, :]
    ubuf[0:c, :] = tail
    tail_ref[...] = tail
    mix = jnp.dot((bgate * y).astype(BF16), wout_ref[...], preferred_element_type=F32)
    o_ref[...] = _layer_norm(alpha * x + mix, gain_ref[...], bias_ref[...]).astype(o_ref.dtype)


def _conv_ln(x, w_in, conv_w, w_out, gain, bias, carry, *, alpha, tm, rows_per_seq):
    rows, d = x.shape
    tm = min(tm, rows_per_seq)
    const = lambda i: (0, 0)
    return pl.pallas_call(
        functools.partial(_conv_ln_kernel, alpha=alpha, tiles_per_seq=rows_per_seq // tm),
        grid=(rows // tm,),
        in_specs=[
            pl.BlockSpec((tm, d), lambda i: (i, 0)),
            pl.BlockSpec(w_in.shape, const),
            pl.BlockSpec(conv_w.shape, const),
            pl.BlockSpec(w_out.shape, const),
            pl.BlockSpec((1, d), const),
            pl.BlockSpec((1, d), const),
            pl.BlockSpec((CONV_CARRY_ROWS, d), const),
        ],
        out_specs=[pl.BlockSpec((tm, d), lambda i: (i, 0)),
                   pl.BlockSpec((CONV_CARRY_ROWS, d), const)],
        out_shape=[jax.ShapeDtypeStruct((rows, d), F32),
                   jax.ShapeDtypeStruct((CONV_CARRY_ROWS, d), F32)],
        scratch_shapes=[pltpu.VMEM((tm + CONV_CARRY_ROWS, d), F32)],
        compiler_params=_params("arbitrary"),
        name="conv_ln",
    )(x, w_in, conv_w, w_out, gain, bias, carry)


def _kv_kernel(x_ref, wkv_ref, wf_ref, fb_ref, cin_ref, k_ref, v_ref, c_ref, auxq_ref, auxk_ref,
               carry_sc, *, tiles_per_seq, n_heads, transpose_v):
    i = pl.program_id(0)
    tm, d = x_ref.shape

    @pl.when(i % tiles_per_seq == 0)
    def _():
        carry_sc[...] = cin_ref[...]

    xb = x_ref[...].astype(BF16)
    kv = jnp.dot(xb, wkv_ref[...], preferred_element_type=F32)
    k_ref[...] = kv[:, 0:d].astype(BF16)
    if transpose_v:
        vt = kv[:, d:2 * d].T.astype(BF16)
        for hh in range(n_heads):
            v_ref[0, hh * V_ROWS:hh * V_ROWS + HEAD_DIM, :] = vt[hh * HEAD_DIM:(hh + 1) * HEAD_DIM, :]
            v_ref[0, hh * V_ROWS + HEAD_DIM:(hh + 1) * V_ROWS, :] = _ones_row_tile(tm)
    else:
        v_ref[...] = kv[:, d:2 * d].astype(BF16)

    f = jnp.dot(xb, wf_ref[...], preferred_element_type=F32) + fb_ref[...]
    log_f = jnp.minimum(f, 0.0) - jnp.log1p(jnp.exp(-jnp.abs(f)))
    row = lax.broadcasted_iota(jnp.int32, log_f.shape, 0)
    csum = log_f
    shift = 1
    while shift < tm:
        csum = csum + jnp.where(row >= shift, pltpu.roll(csum, shift, axis=0), 0.0)
        shift *= 2
    csum = csum + carry_sc[...]
    carry_sc[...] = csum[tm - 1:tm, :]
    c_ref[...] = csum

    lane = lax.broadcasted_iota(jnp.int32, csum.shape, 1)
    c2 = jnp.where(lane < n_heads, csum * LOG2E, 0.0)
    hi = c2.astype(BF16).astype(F32)
    r1 = c2 - hi
    mid = r1.astype(BF16).astype(F32)
    lo = r1 - mid
    aux = hi + pltpu.roll(mid, HEAD_GROUP, axis=1) + pltpu.roll(lo, 2 * HEAD_GROUP, axis=1)
    auxq_ref[...] = aux.astype(BF16)
    auxk_ref[...] = (-pltpu.roll(aux, AUX_K_LANE - AUX_Q_LANE, axis=1)).astype(BF16)


def _shared_kv(x, w_kv, w_f, f_bias, c_in, *, tm, rows_per_seq, n_heads, transpose_v):
    rows, d = x.shape
    tm = min(tm, rows_per_seq)
    const = lambda i: (0, 0)
    row_tile = lambda i: (i, 0)
    if transpose_v:
        v_spec = pl.BlockSpec((1, n_heads * V_ROWS, tm), lambda i: (i, 0, 0))
        v_shape = jax.ShapeDtypeStruct((rows // tm, n_heads * V_ROWS, tm), BF16)
    else:
        v_spec = pl.BlockSpec((tm, d), row_tile)
        v_shape = jax.ShapeDtypeStruct((rows, d), BF16)
    return pl.pallas_call(
        functools.partial(_kv_kernel, tiles_per_seq=rows_per_seq // tm, n_heads=n_heads,
                          transpose_v=transpose_v),
        grid=(rows // tm,),
        in_specs=[
            pl.BlockSpec((tm, d), row_tile),
            pl.BlockSpec(w_kv.shape, const),
            pl.BlockSpec(w_f.shape, const),
            pl.BlockSpec((1, LANES), const),
            pl.BlockSpec((1, LANES), const),
        ],
        out_specs=[pl.BlockSpec((tm, d), row_tile), v_spec,
                   pl.BlockSpec((tm, LANES), row_tile),
                   pl.BlockSpec((tm, LANES), row_tile),
                   pl.BlockSpec((tm, LANES), row_tile)],
        out_shape=[jax.ShapeDtypeStruct((rows, d), BF16), v_shape,
                   jax.ShapeDtypeStruct((rows, LANES), F32),
                   jax.ShapeDtypeStruct((rows, LANES), BF16),
                   jax.ShapeDtypeStruct((rows, LANES), BF16)],
        scratch_shapes=[pltpu.VMEM((1, LANES), F32)],
        compiler_params=_params("arbitrary"),
        name="shared_kv",
    )(x, w_kv, w_f, f_bias, c_in)


def _qproj_kernel(x_ref, w_ref, q_ref, *, scale):
    q = jnp.dot(x_ref[...].astype(BF16), w_ref[...], preferred_element_type=F32)
    q_ref[...] = (q * scale).astype(q_ref.dtype)


def _qproj(x, w, *, scale, tm):
    rows, d = x.shape
    return pl.pallas_call(
        functools.partial(_qproj_kernel, scale=scale),
        grid=(rows // tm,),
        in_specs=[pl.BlockSpec((tm, d), lambda i: (i, 0)),
                  pl.BlockSpec(w.shape, lambda i: (0, 0))],
        out_specs=pl.BlockSpec((tm, w.shape[1]), lambda i: (i, 0)),
        out_shape=jax.ShapeDtypeStruct((rows, w.shape[1]), BF16),
        compiler_params=_params("parallel"),
        name="qproj",
    )(x, w)


def _head_one_hot(h, base):
    lane = lax.broadcasted_iota(jnp.int32, (1, LANES), 1)
    hit = (lane == base + h) | (lane == base + HEAD_GROUP + h) | (lane == base + 2 * HEAD_GROUP + h)
    return jnp.where(hit, 1.0, 0.0).astype(BF16)


def _attn_kernel(q_ref, auxq_ref, k_ref, auxk_ref, vt_ref, km_ref, auxkm_ref, vtm_ref, o_ref,
                 qa_sc, s_sc, m_sc, acc_sc, *, tk):
    h = pl.program_id(1)
    qi = pl.program_id(2)
    tq = q_ref.shape[0]
    ts = tk
    n_strips = tq // ts
    nt = (((1,), (1,)), ((), ()))

    q_sel = _head_one_hot(h, AUX_K_LANE)
    k_sel = _head_one_hot(h, AUX_Q_LANE)
    qa_sc[:, 0:HEAD_DIM] = q_ref[...]
    qa_sc[:, HEAD_DIM:] = auxq_ref[...] + q_sel
    m_sc[...] = jnp.full(m_sc.shape, -MASK_VALUE, F32)
    acc_sc[...] = jnp.zeros(acc_sc.shape, F32)

    def strip_scores(t, ka, c, causal):
        cs = slice(c * ts, (c + 1) * ts)
        s = lax.dot_general(ka, qa_sc[cs, :], nt, preferred_element_type=F32)
        if causal:
            key = lax.broadcasted_iota(jnp.int32, s.shape, 0)
            qry = lax.broadcasted_iota(jnp.int32, s.shape, 1)
            s = jnp.where(key <= qry, s, -MASK_VALUE)
        s_sc[t % SCORE_SLOTS, 0:s.shape[0], :] = s
        return jnp.max(s, axis=0, keepdims=True)

    def strip_update(t, s_max, vt, c):
        cs = slice(c * ts, (c + 1) * ts)
        m = m_sc[:, cs]
        m_new = jnp.maximum(m, s_max)
        corr = jnp.exp2(m - m_new)
        m_sc[:, cs] = m_new
        p = jnp.exp2(s_sc[t % SCORE_SLOTS, 0:vt.shape[1], :] - m_new)
        pv = jnp.dot(vt, p.astype(BF16), preferred_element_type=F32)
        acc_sc[:, cs] = corr * acc_sc[:, cs] + pv

    def run_tasks(tasks):
        pending = [strip_scores(t, ka, c, causal)
                   for t, (ka, _, c, causal) in enumerate(tasks[:QK_LOOKAHEAD])]
        for t, (_, vt, c, _) in enumerate(tasks):
            if t + QK_LOOKAHEAD < len(tasks):
                ka_n, _, c_n, causal_n = tasks[t + QK_LOOKAHEAD]
                pending.append(strip_scores(t + QK_LOOKAHEAD, ka_n, c_n, causal_n))
            strip_update(t, pending.pop(0), vt, c)

    def key_block(j):
        off = pl.multiple_of(j * tk, tk)
        ka = jnp.concatenate([k_ref[pl.ds(off, tk), :], auxk_ref[pl.ds(off, tk), :] + k_sel], axis=1)
        return ka, vt_ref[j]

    ka_m = jnp.concatenate([km_ref[...], auxkm_ref[...] + k_sel], axis=1)
    run_tasks([(ka_m, vtm_ref[...], c, False) for c in range(n_strips)])

    def full_blocks(jj, _):
        blocks = [key_block(jj * KV_UNROLL + u) for u in range(KV_UNROLL)]
        run_tasks([(ka, vt, c, False) for ka, vt in blocks for c in range(n_strips)])
        return 0

    assert n_strips % KV_UNROLL == 0
    lax.fori_loop(0, qi * (n_strips // KV_UNROLL), full_blocks, 0)

    diag = [key_block(qi * n_strips + d) for d in range(n_strips)]
    run_tasks([(diag[d][0], diag[d][1], c, c == d) for d in range(n_strips) for c in range(d, n_strips)])

    denom = acc_sc[HEAD_DIM:HEAD_DIM + 1, :]
    o_ref[...] = (acc_sc[0:HEAD_DIM, :] / denom).T.astype(o_ref.dtype)


def _attention(q, auxq, k, auxk, vt, km, auxkm, vtm, *, batch, seq, n_heads, tq, tk):
    rows, d = q.shape
    nq = seq // tq
    nk = seq // tk
    assert vt.shape == (batch * nk, n_heads * V_ROWS, tk)
    return pl.pallas_call(
        functools.partial(_attn_kernel, tk=tk),
        grid=(batch, n_heads, nq),
        in_specs=[
            pl.BlockSpec((tq, HEAD_DIM), lambda b, h, i: (b * nq + i, h)),
            pl.BlockSpec((tq, LANES), lambda b, h, i: (b * nq + i, 0)),
            pl.BlockSpec((seq, HEAD_DIM), lambda b, h, i: (b, h)),
            pl.BlockSpec((seq, LANES), lambda b, h, i: (b, 0)),
            pl.BlockSpec((nk, V_ROWS, tk), lambda b, h, i: (b, h, 0)),
            pl.BlockSpec((LANES, HEAD_DIM), lambda b, h, i: (0, h)),
            pl.BlockSpec((LANES, LANES), lambda b, h, i: (0, 0)),
            pl.BlockSpec((V_ROWS, LANES), lambda b, h, i: (h, 0)),
        ],
        out_specs=pl.BlockSpec((tq, HEAD_DIM), lambda b, h, i: (b * nq + i, h)),
        out_shape=jax.ShapeDtypeStruct((rows, d), BF16),
        scratch_shapes=[pltpu.VMEM((tq, HEAD_DIM + LANES), BF16),
                        pltpu.VMEM((SCORE_SLOTS, tk, tk), F32),
                        pltpu.VMEM((1, tq), F32),
                        pltpu.VMEM((V_ROWS, tq), F32)],
        compiler_params=_params("parallel", "parallel", "arbitrary"),
        name="fox_attention",
    )(q, auxq, k, auxk, vt, km, auxkm, vtm)


def _oproj_ln_kernel(o_ref, x_ref, w_ref, gain_ref, bias_ref, y_ref, *, alpha):
    mix = jnp.dot(o_ref[...], w_ref[...], preferred_element_type=F32)
    y_ref[...] = _layer_norm(alpha * x_ref[...] + mix, gain_ref[...], bias_ref[...]).astype(y_ref.dtype)


def _oproj_ln(o, x, w, gain, bias, *, alpha, tm):
    rows, d = x.shape
    const = lambda i: (0, 0)
    return pl.pallas_call(
        functools.partial(_oproj_ln_kernel, alpha=alpha),
        grid=(rows // tm,),
        in_specs=[pl.BlockSpec((tm, d), lambda i: (i, 0)),
                  pl.BlockSpec((tm, d), lambda i: (i, 0)),
                  pl.BlockSpec(w.shape, const),
                  pl.BlockSpec((1, d), const),
                  pl.BlockSpec((1, d), const)],
        out_specs=pl.BlockSpec((tm, d), lambda i: (i, 0)),
        out_shape=jax.ShapeDtypeStruct((rows, d), F32),
        compiler_params=_params("parallel"),
        name="oproj_ln",
    )(o, x, w, gain, bias)


def kernel(x, meta, ffn1_wg, ffn1_wu, ffn1_wd, ffn2_wg, ffn2_wu, ffn2_wd, ln_gain, ln_bias,
           conv_w_in, conv_w, conv_w_out, kv_w, f_bias, attn_w_q, attn_w_o):
    batch, seq, d = x.shape
    n_meta = meta.shape[0]
    depth = ffn1_wg.shape[0]
    n_heads = f_bias.shape[0]
    assert depth == 2 and conv_w_in.shape[0] == 1 and attn_w_q.shape[0] == 1
    assert d == n_heads * HEAD_DIM and n_heads == HEAD_GROUP
    assert n_meta <= LANES and n_meta % 16 == 0
    alpha = (2 * depth) ** 0.25
    q_scale = LOG2E / math.sqrt(HEAD_DIM)
    attn_tile = 512

    bf = lambda w: w.astype(BF16)
    gain = lambda l, i: ln_gain[l, i].reshape(1, d).astype(F32)
    bias = lambda l, i: ln_bias[l, i].reshape(1, d).astype(F32)
    def ffn_weights(wg, wu, wd):
        dff = wg.shape[1]
        n_chunks = dff // FFN_CHUNK
        wgu = jnp.concatenate([wg.reshape(d, n_chunks, FFN_CHUNK), wu.reshape(d, n_chunks, FFN_CHUNK)],
                              axis=2)
        return bf(wgu.transpose(1, 0, 2)), bf(0.5 * wd)

    ffn_w = {(1, l): ffn_weights(ffn1_wg[l], ffn1_wu[l], ffn1_wd[l]) for l in range(depth)}
    ffn_w.update({(2, l): ffn_weights(ffn2_wg[l], ffn2_wu[l], ffn2_wd[l]) for l in range(depth)})

    def ffn(h, which, l, out_dtype=F32):
        wgu, wd = ffn_w[(which, l)]
        return _ffn_ln(h, wgu, wd, gain(l, 0 if which == 1 else 2), bias(l, 0 if which == 1 else 2),
                       alpha=alpha, tm=1024, out_dtype=out_dtype)

    xs = x.reshape(batch * seq, d)
    ms = meta.astype(x.dtype)

    w_in, w_out, cw = bf(conv_w_in[0]), bf(conv_w_out[0]), conv_w[0].astype(F32)
    conv = functools.partial(_conv_ln, alpha=alpha, tm=512)
    m1 = ffn(ms, 1, 0)
    h1 = ffn(xs, 1, 0)
    zero_carry = jnp.zeros((CONV_CARRY_ROWS, d), F32)
    m2, m_tail = conv(m1, w_in, cw, w_out, gain(0, 1), bias(0, 1), zero_carry, rows_per_seq=n_meta)
    h2, _ = conv(h1, w_in, cw, w_out, gain(0, 1), bias(0, 1), m_tail, rows_per_seq=seq)
    m3 = ffn(m2, 2, 0)
    h3 = ffn(h2, 2, 0)

    w_kv = bf(kv_w[:, :2 * d])
    w_f = jnp.pad(bf(kv_w[:, 2 * d:]), ((0, 0), (0, LANES - n_heads)))
    fb = jnp.pad(f_bias.astype(F32), (0, LANES - n_heads)).reshape(1, LANES)
    kv = functools.partial(_shared_kv, tm=attn_tile, n_heads=n_heads)
    km, vm, cm, _, auxkm = kv(m3, w_kv, w_f, fb, jnp.zeros((1, LANES), F32),
                              rows_per_seq=n_meta, transpose_v=False)
    k, vt, _, auxq, auxk = kv(h3, w_kv, w_f, fb, cm[n_meta - 1:n_meta],
                              rows_per_seq=seq, transpose_v=True)
    pad_aux = np.zeros((LANES - n_meta, LANES), np.float32)
    pad_aux[:, AUX_K_LANE:AUX_K_LANE + n_heads] = -MASK_VALUE
    km = jnp.pad(km, ((0, LANES - n_meta), (0, 0)))
    auxkm = jnp.concatenate([auxkm, jnp.asarray(pad_aux, BF16)], axis=0)
    ones_tile = np.zeros((n_heads, BF16_SUBLANES, LANES), np.float32)
    ones_tile[:, 0, :] = 1.0
    vtm = jnp.pad(vm.T, ((0, 0), (0, LANES - n_meta))).reshape(n_heads, HEAD_DIM, LANES)
    vtm = jnp.concatenate([vtm, jnp.asarray(ones_tile, BF16)], axis=1).reshape(n_heads * V_ROWS, LANES)

    h4 = ffn(h3, 1, 1)
    q = _qproj(h4, bf(attn_w_q[0]), scale=q_scale, tm=1024)
    o = _attention(q, auxq, k, auxk, vt, km, auxkm, vtm, batch=batch, seq=seq, n_heads=n_heads,
                   tq=4 * attn_tile, tk=attn_tile)
    h5 = _oproj_ln(o, h4, bf(attn_w_o[0]), gain(1, 1), bias(1, 1), alpha=alpha, tm=1024)
    out = ffn(h5, 2, 1, out_dtype=x.dtype)
    return out.reshape(batch, seq, d)
```

```python
import functools
import math

import jax
import jax.numpy as jnp
import numpy as np
from jax import lax
from jax.experimental import pallas as pl
from jax.experimental.pallas import tpu as pltpu

F32 = jnp.float32
BF16 = jnp.bfloat16

LN_EPS = 1e-5
MASK_VALUE = 1e30
LOG2E = math.log2(math.e)
HEAD_DIM = 128
LANES = 128
CONV_CARRY_ROWS = 8
FFN_CHUNK = 256
VMEM_LIMIT_BYTES = 56 * 1024 * 1024

N_SPLIT = 3
HEAD_GROUP = 8
AUX_Q_LANE = 0
AUX_K_LANE = N_SPLIT * HEAD_GROUP
QK_LOOKAHEAD = 2
SCORE_SLOTS = QK_LOOKAHEAD + 2
KV_UNROLL = 2
BF16_SUBLANES = 16
V_ROWS = HEAD_DIM + BF16_SUBLANES


def _ones_row_tile(width):
    row = lax.broadcasted_iota(jnp.int32, (BF16_SUBLANES, width), 0)
    return jnp.where(row == 0, 1.0, 0.0).astype(BF16)


def _params(*semantics):
    return pltpu.CompilerParams(dimension_semantics=semantics, vmem_limit_bytes=VMEM_LIMIT_BYTES)


def _layer_norm(y, gain, bias):
    mu = jnp.mean(y, axis=-1, keepdims=True)
    yc = y - mu
    var = jnp.mean(yc * yc, axis=-1, keepdims=True)
    return yc * lax.rsqrt(var + LN_EPS) * gain + bias


def _ffn_ln_kernel(*refs, alpha, n_chunks, n_side):
    if n_side:
        (x_ref, side_ref, wg_ref, wu_ref, wd_ref, gain_ref, bias_ref, o_ref, oside_ref,
         wgu_sc, wd_sc, xb_sc, gu_sc, a_sc) = refs
    else:
        (x_ref, wg_ref, wu_ref, wd_ref, gain_ref, bias_ref, o_ref,
         wgu_sc, wd_sc, xb_sc, gu_sc, a_sc) = refs
    i = pl.program_id(0)
    tf = wg_ref.shape[1]

    @pl.when(i < n_chunks)
    def _stage_weights():
        wgu_sc[i, :, 0:tf] = wg_ref[...].astype(BF16)
        wgu_sc[i, :, tf:2 * tf] = wu_ref[...].astype(BF16)
        wd_sc[pl.ds(pl.multiple_of(i * tf, tf), tf), :] = (0.5 * wd_ref[...]).astype(BF16)

    def ffn_rows(src_ref, dst_ref, r):
        xb_sc[0:r, :] = src_ref[...].astype(BF16)

        def gate_up(c):
            gu_sc[c % 2, 0:r, :] = jnp.dot(xb_sc[0:r, :], wgu_sc[c], preferred_element_type=F32)

        gate_up(0)
        for c in range(n_chunks):
            if c + 1 < n_chunks:
                gate_up(c + 1)
            g = gu_sc[c % 2, 0:r, 0:tf]
            u = gu_sc[c % 2, 0:r, tf:2 * tf]
            a_sc[0:r, c * tf:(c + 1) * tf] = (g * jax.nn.sigmoid(g) * u).astype(BF16)
        half_ffn = jnp.dot(a_sc[0:r, :], wd_sc[...], preferred_element_type=F32)
        y = alpha * src_ref[...].astype(F32) + half_ffn
        dst_ref[...] = _layer_norm(y, gain_ref[...], bias_ref[...]).astype(dst_ref.dtype)

    if n_side:
        @pl.when(i == n_chunks)
        def _side_rows():
            ffn_rows(side_ref, oside_ref, n_side)

    @pl.when(i >= n_chunks)
    def _row_tile():
        ffn_rows(x_ref, o_ref, x_ref.shape[0])


def _ffn_ln(x, side, wg, wu, wd, layer, gain, bias, *, alpha, tm, out_dtype):
    rows, d = x.shape
    dff = wg.shape[2]
    tf = FFN_CHUNK
    n_chunks = dff // tf
    n_side = 0 if side is None else side.shape[0]
    chunk = lambda i: jnp.minimum(i, n_chunks - 1)
    tile = lambda i: (jnp.maximum(i - n_chunks, 0), 0)
    const = lambda i: (0, 0)
    in_specs = [pl.BlockSpec((tm, d), tile)]
    out_specs = [pl.BlockSpec((tm, d), tile)]
    out_shape = [jax.ShapeDtypeStruct((rows, d), out_dtype)]
    operands = [x]
    if n_side:
        in_specs.append(pl.BlockSpec((n_side, d), const))
        out_specs.append(pl.BlockSpec((n_side, d), const))
        out_shape.append(jax.ShapeDtypeStruct((n_side, d), out_dtype))
        operands.append(side)
    in_specs += [
        pl.BlockSpec((None, d, tf), lambda i: (layer, 0, chunk(i))),
        pl.BlockSpec((None, d, tf), lambda i: (layer, 0, chunk(i))),
        pl.BlockSpec((None, tf, d), lambda i: (layer, chunk(i), 0)),
        pl.BlockSpec((1, d), const),
        pl.BlockSpec((1, d), const),
    ]
    outs = pl.pallas_call(
        functools.partial(_ffn_ln_kernel, alpha=alpha, n_chunks=n_chunks, n_side=n_side),
        grid=(n_chunks + rows // tm,),
        in_specs=in_specs,
        out_specs=out_specs,
        out_shape=out_shape,
        scratch_shapes=[pltpu.VMEM((n_chunks, d, 2 * tf), BF16), pltpu.VMEM((dff, d), BF16),
                        pltpu.VMEM((tm, d), BF16), pltpu.VMEM((2, tm, 2 * tf), F32),
                        pltpu.VMEM((tm, dff), BF16)],
        compiler_params=_params("arbitrary"),
        name="ffn_ln",
    )(*operands, wg, wu, wd, gain, bias)
    return outs if n_side else (outs[0], None)


def _conv_ln_kernel(x_ref, win_ref, cw_ref, wout_ref, gain_ref, bias_ref, carry_ref,
                    o_ref, tail_ref, ubuf, *, alpha, tiles_per_seq):
    i = pl.program_id(0)
    tm, d = x_ref.shape
    c = CONV_CARRY_ROWS

    @pl.when(i % tiles_per_seq == 0)
    def _():
        ubuf[0:c, :] = carry_ref[...]

    x = x_ref[...]
    proj = jnp.dot(x.astype(BF16), win_ref[...], preferred_element_type=F32)
    bgate = proj[:, 0:d]
    u = proj[:, d:2 * d] * proj[:, 2 * d:3 * d]
    ubuf[c:c + tm, :] = u
    cw = cw_ref[...]
    y = (cw[2:3, :] * u
         + cw[1:2, :] * ubuf[c - 1:c - 1 + tm, :]
         + cw[0:1, :] * ubuf[c - 2:c - 2 + tm, :])
    tail = ubuf[tm:tm + c, :]
    ubuf[0:c, :] = tail
    tail_ref[...] = tail
    mix = jnp.dot((bgate * y).astype(BF16), wout_ref[...], preferred_element_type=F32)
    o_ref[...] = _layer_norm(alpha * x + mix, gain_ref[...], bias_ref[...]).astype(o_ref.dtype)


def _conv_ln(x, w_in, conv_w, w_out, gain, bias, carry, *, alpha, tm, rows_per_seq):
    rows, d = x.shape
    tm = min(tm, rows_per_seq)
    const = lambda i: (0, 0)
    return pl.pallas_call(
        functools.partial(_conv_ln_kernel, alpha=alpha, tiles_per_seq=rows_per_seq // tm),
        grid=(rows // tm,),
        in_specs=[
            pl.BlockSpec((tm, d), lambda i: (i, 0)),
            pl.BlockSpec(w_in.shape, const),
            pl.BlockSpec(conv_w.shape, const),
            pl.BlockSpec(w_out.shape, const),
            pl.BlockSpec((1, d), const),
            pl.BlockSpec((1, d), const),
            pl.BlockSpec((CONV_CARRY_ROWS, d), const),
        ],
        out_specs=[pl.BlockSpec((tm, d), lambda i: (i, 0)),
                   pl.BlockSpec((CONV_CARRY_ROWS, d), const)],
        out_shape=[jax.ShapeDtypeStruct((rows, d), F32),
                   jax.ShapeDtypeStruct((CONV_CARRY_ROWS, d), F32)],
        scratch_shapes=[pltpu.VMEM((tm + CONV_CARRY_ROWS, d), F32)],
        compiler_params=_params("arbitrary"),
        name="conv_ln",
    )(x, w_in, conv_w, w_out, gain, bias, carry)


def _kv_kernel(x_ref, wkv_ref, wf_ref, fb_ref, cin_ref, k_ref, v_ref, c_ref, auxq_ref, auxk_ref,
               carry_sc, *, tiles_per_seq, n_heads, transpose_v):
    i = pl.program_id(0)
    tm, d = x_ref.shape

    @pl.when(i % tiles_per_seq == 0)
    def _():
        carry_sc[...] = cin_ref[...]

    xb = x_ref[...].astype(BF16)
    kv = jnp.dot(xb, wkv_ref[...], preferred_element_type=F32)
    k_ref[...] = kv[:, 0:d].astype(BF16)
    if transpose_v:
        vt = kv[:, d:2 * d].T.astype(BF16)
        for hh in range(n_heads):
            v_ref[0, hh * V_ROWS:hh * V_ROWS + HEAD_DIM, :] = vt[hh * HEAD_DIM:(hh + 1) * HEAD_DIM, :]
            v_ref[0, hh * V_ROWS + HEAD_DIM:(hh + 1) * V_ROWS, :] = _ones_row_tile(tm)
    else:
        v_ref[...] = kv[:, d:2 * d].astype(BF16)

    f = jnp.dot(xb, wf_ref[...], preferred_element_type=F32) + fb_ref[...]
    log_f = jnp.minimum(f, 0.0) - jnp.log1p(jnp.exp(-jnp.abs(f)))
    row = lax.broadcasted_iota(jnp.int32, log_f.shape, 0)
    csum = log_f
    shift = 1
    while shift < tm:
        csum = csum + jnp.where(row >= shift, pltpu.roll(csum, shift, axis=0), 0.0)
        shift *= 2
    csum = csum + carry_sc[...]
    carry_sc[...] = csum[tm - 1:tm, :]
    c_ref[...] = csum

    lane = lax.broadcasted_iota(jnp.int32, csum.shape, 1)
    c2 = jnp.where(lane < n_heads, csum * LOG2E, 0.0)
    hi = c2.astype(BF16).astype(F32)
    r1 = c2 - hi
    mid = r1.astype(BF16).astype(F32)
    lo = r1 - mid
    aux = hi + pltpu.roll(mid, HEAD_GROUP, axis=1) + pltpu.roll(lo, 2 * HEAD_GROUP, axis=1)
    auxq_ref[...] = aux.astype(BF16)
    auxk_ref[...] = (-pltpu.roll(aux, AUX_K_LANE - AUX_Q_LANE, axis=1)).astype(BF16)


def _shared_kv(x, w_kv, w_f, f_bias, c_in, *, tm, rows_per_seq, n_heads, transpose_v):
    rows, d = x.shape
    tm = min(tm, rows_per_seq)
    const = lambda i: (0, 0)
    row_tile = lambda i: (i, 0)
    if transpose_v:
        v_spec = pl.BlockSpec((1, n_heads * V_ROWS, tm), lambda i: (i, 0, 0))
        v_shape = jax.ShapeDtypeStruct((rows // tm, n_heads * V_ROWS, tm), BF16)
    else:
        v_spec = pl.BlockSpec((tm, d), row_tile)
        v_shape = jax.ShapeDtypeStruct((rows, d), BF16)
    return pl.pallas_call(
        functools.partial(_kv_kernel, tiles_per_seq=rows_per_seq // tm, n_heads=n_heads,
                          transpose_v=transpose_v),
        grid=(rows // tm,),
        in_specs=[
            pl.BlockSpec((tm, d), row_tile),
            pl.BlockSpec(w_kv.shape, const),
            pl.BlockSpec(w_f.shape, const),
            pl.BlockSpec((1, LANES), const),
            pl.BlockSpec((1, LANES), const),
        ],
        out_specs=[pl.BlockSpec((tm, d), row_tile), v_spec,
                   pl.BlockSpec((tm, LANES), row_tile),
                   pl.BlockSpec((tm, LANES), row_tile),
                   pl.BlockSpec((tm, LANES), row_tile)],
        out_shape=[jax.ShapeDtypeStruct((rows, d), BF16), v_shape,
                   jax.ShapeDtypeStruct((rows, LANES), F32),
                   jax.ShapeDtypeStruct((rows, LANES), BF16),
                   jax.ShapeDtypeStruct((rows, LANES), BF16)],
        scratch_shapes=[pltpu.VMEM((1, LANES), F32)],
        compiler_params=_params("arbitrary"),
        name="shared_kv",
    )(x, w_kv, w_f, f_bias, c_in)


def _qproj_kernel(x_ref, w_ref, q_ref, *, scale):
    q = jnp.dot(x_ref[...].astype(BF16), w_ref[...], preferred_element_type=F32)
    q_ref[...] = (q * scale).astype(q_ref.dtype)


def _qproj(x, w, *, scale, tm):
    rows, d = x.shape
    return pl.pallas_call(
        functools.partial(_qproj_kernel, scale=scale),
        grid=(rows // tm,),
        in_specs=[pl.BlockSpec((tm, d), lambda i: (i, 0)),
                  pl.BlockSpec(w.shape, lambda i: (0, 0))],
        out_specs=pl.BlockSpec((tm, w.shape[1]), lambda i: (i, 0)),
        out_shape=jax.ShapeDtypeStruct((rows, w.shape[1]), BF16),
        compiler_params=_params("parallel"),
        name="qproj",
    )(x, w)


def _head_one_hot(h, base):
    lane = lax.broadcasted_iota(jnp.int32, (1, LANES), 1)
    hit = (lane == base + h) | (lane == base + HEAD_GROUP + h) | (lane == base + 2 * HEAD_GROUP + h)
    return jnp.where(hit, 1.0, 0.0).astype(BF16)


def _attn_kernel(q_ref, auxq_ref, k_ref, auxk_ref, vt_ref, km_ref, auxkm_ref, vtm_ref, o_ref,
                 qa_sc, s_sc, m_sc, acc_sc, *, tk):
    h = pl.program_id(1)
    qi = pl.program_id(2)
    tq = q_ref.shape[0]
    ts = tk
    n_strips = tq // ts
    nt = (((1,), (1,)), ((), ()))

    q_sel = _head_one_hot(h, AUX_K_LANE)
    k_sel = _head_one_hot(h, AUX_Q_LANE)
    qa_sc[:, 0:HEAD_DIM] = q_ref[...]
    qa_sc[:, HEAD_DIM:] = auxq_ref[...] + q_sel
    m_sc[...] = jnp.full(m_sc.shape, -MASK_VALUE, F32)
    acc_sc[...] = jnp.zeros(acc_sc.shape, F32)

    def strip_scores(t, ka, c, causal):
        cs = slice(c * ts, (c + 1) * ts)
        s = lax.dot_general(ka, qa_sc[cs, :], nt, preferred_element_type=F32)
        if causal:
            key = lax.broadcasted_iota(jnp.int32, s.shape, 0)
            qry = lax.broadcasted_iota(jnp.int32, s.shape, 1)
            s = jnp.where(key <= qry, s, -MASK_VALUE)
        s_sc[t % SCORE_SLOTS, 0:s.shape[0], :] = s
        return jnp.max(s, axis=0, keepdims=True)

    def strip_update(t, s_max, vt, c):
        cs = slice(c * ts, (c + 1) * ts)
        m = m_sc[:, cs]
        m_new = jnp.maximum(m, s_max)
        corr = jnp.exp2(m - m_new)
        m_sc[:, cs] = m_new
        p = jnp.exp2(s_sc[t % SCORE_SLOTS, 0:vt.shape[1], :] - m_new)
        pv = jnp.dot(vt, p.astype(BF16), preferred_element_type=F32)
        acc_sc[:, cs] = corr * acc_sc[:, cs] + pv

    def run_tasks(tasks):
        pending = [strip_scores(t, ka, c, causal)
                   for t, (ka, _, c, causal) in enumerate(tasks[:QK_LOOKAHEAD])]
        for t, (_, vt, c, _) in enumerate(tasks):
            if t + QK_LOOKAHEAD < len(tasks):
                ka_n, _, c_n, causal_n = tasks[t + QK_LOOKAHEAD]
                pending.append(strip_scores(t + QK_LOOKAHEAD, ka_n, c_n, causal_n))
            strip_update(t, pending.pop(0), vt, c)

    def key_block(j):
        off = pl.multiple_of(j * tk, tk)
        ka = jnp.concatenate([k_ref[pl.ds(off, tk), :], auxk_ref[pl.ds(off, tk), :] + k_sel], axis=1)
        return ka, vt_ref[j]

    ka_m = jnp.concatenate([km_ref[...], auxkm_ref[...] + k_sel], axis=1)
    run_tasks([(ka_m, vtm_ref[...], c, False) for c in range(n_strips)])

    def full_blocks(jj, _):
        blocks = [key_block(jj * KV_UNROLL + u) for u in range(KV_UNROLL)]
        run_tasks([(ka, vt, c, False) for ka, vt in blocks for c in range(n_strips)])
        return 0

    assert n_strips % KV_UNROLL == 0
    lax.fori_loop(0, qi * (n_strips // KV_UNROLL), full_blocks, 0)

    diag = [key_block(qi * n_strips + d) for d in range(n_strips)]
    run_tasks([(diag[d][0], diag[d][1], c, c == d) for d in range(n_strips) for c in range(d, n_strips)])

    denom = acc_sc[HEAD_DIM:HEAD_DIM + 1, :]
    o_ref[...] = (acc_sc[0:HEAD_DIM, :] / denom).T.astype(o_ref.dtype)


def _attention(q, auxq, k, auxk, vt, km, auxkm, vtm, *, batch, seq, n_heads, tq, tk):
    rows, d = q.shape
    nq = seq // tq
    nk = seq // tk
    assert vt.shape == (batch * nk, n_heads * V_ROWS, tk)
    return pl.pallas_call(
        functools.partial(_attn_kernel, tk=tk),
        grid=(batch, n_heads, nq),
        in_specs=[
            pl.BlockSpec((tq, HEAD_DIM), lambda b, h, i: (b * nq + i, h)),
            pl.BlockSpec((tq, LANES), lambda b, h, i: (b * nq + i, 0)),
            pl.BlockSpec((seq, HEAD_DIM), lambda b, h, i: (b, h)),
            pl.BlockSpec((seq, LANES), lambda b, h, i: (b, 0)),
            pl.BlockSpec((nk, V_ROWS, tk), lambda b, h, i: (b, h, 0)),
            pl.BlockSpec((LANES, HEAD_DIM), lambda b, h, i: (0, h)),
            pl.BlockSpec((LANES, LANES), lambda b, h, i: (0, 0)),
            pl.BlockSpec((V_ROWS, LANES), lambda b, h, i: (h, 0)),
        ],
        out_specs=pl.BlockSpec((tq, HEAD_DIM), lambda b, h, i: (b * nq + i, h)),
        out_shape=jax.ShapeDtypeStruct((rows, d), BF16),
        scratch_shapes=[pltpu.VMEM((tq, HEAD_DIM + LANES), BF16),
                        pltpu.VMEM((SCORE_SLOTS, tk, tk), F32),
                        pltpu.VMEM((1, tq), F32),
                        pltpu.VMEM((V_ROWS, tq), F32)],
        compiler_params=_params("parallel", "parallel", "arbitrary"),
        name="fox_attention",
    )(q, auxq, k, auxk, vt, km, auxkm, vtm)


def _oproj_ln_kernel(o_ref, x_ref, w_ref, gain_ref, bias_ref, y_ref, *, alpha):
    mix = jnp.dot(o_ref[...], w_ref[...], preferred_element_type=F32)
    y_ref[...] = _layer_norm(alpha * x_ref[...] + mix, gain_ref[...], bias_ref[...]).astype(y_ref.dtype)


def _oproj_ln(o, x, w, gain, bias, *, alpha, tm):
    rows, d = x.shape
    const = lambda i: (0, 0)
    return pl.pallas_call(
        functools.partial(_oproj_ln_kernel, alpha=alpha),
        grid=(rows // tm,),
        in_specs=[pl.BlockSpec((tm, d), lambda i: (i, 0)),
                  pl.BlockSpec((tm, d), lambda i: (i, 0)),
                  pl.BlockSpec(w.shape, const),
                  pl.BlockSpec((1, d), const),
                  pl.BlockSpec((1, d), const)],
        out_specs=pl.BlockSpec((tm, d), lambda i: (i, 0)),
        out_shape=jax.ShapeDtypeStruct((rows, d), F32),
        compiler_params=_params("parallel"),
        name="oproj_ln",
    )(o, x, w, gain, bias)


def kernel(x, meta, ffn1_wg, ffn1_wu, ffn1_wd, ffn2_wg, ffn2_wu, ffn2_wd, ln_gain, ln_bias,
           conv_w_in, conv_w, conv_w_out, kv_w, f_bias, attn_w_q, attn_w_o):
    batch, seq, d = x.shape
    n_meta = meta.shape[0]
    depth = ffn1_wg.shape[0]
    n_heads = f_bias.shape[0]
    assert depth == 2 and conv_w_in.shape[0] == 1 and attn_w_q.shape[0] == 1
    assert d == n_heads * HEAD_DIM and n_heads == HEAD_GROUP
    assert n_meta <= LANES and n_meta % 16 == 0
    alpha = (2 * depth) ** 0.25
    q_scale = LOG2E / math.sqrt(HEAD_DIM)
    attn_tile = 512

    bf = lambda w: w.astype(BF16)
    gain = lambda l, i: ln_gain[l, i].reshape(1, d).astype(F32)
    bias = lambda l, i: ln_bias[l, i].reshape(1, d).astype(F32)
    ffn_w = {1: (ffn1_wg, ffn1_wu, ffn1_wd), 2: (ffn2_wg, ffn2_wu, ffn2_wd)}

    def ffn(h, side, which, l, out_dtype=F32):
        ln_idx = 0 if which == 1 else 2
        return _ffn_ln(h, side, *ffn_w[which], l, gain(l, ln_idx), bias(l, ln_idx),
                       alpha=alpha, tm=1024, out_dtype=out_dtype)

    xs = x.reshape(batch * seq, d)
    ms = meta.astype(x.dtype)

    w_in, w_out, cw = bf(conv_w_in[0]), bf(conv_w_out[0]), conv_w[0].astype(F32)
    conv = functools.partial(_conv_ln, alpha=alpha, tm=512)
    h1, m1 = ffn(xs, ms, 1, 0)
    zero_carry = jnp.zeros((CONV_CARRY_ROWS, d), F32)
    m2, m_tail = conv(m1, w_in, cw, w_out, gain(0, 1), bias(0, 1), zero_carry, rows_per_seq=n_meta)
    h2, _ = conv(h1, w_in, cw, w_out, gain(0, 1), bias(0, 1), m_tail, rows_per_seq=seq)
    h3, m3 = ffn(h2, m2, 2, 0)

    w_kv = bf(kv_w[:, :2 * d])
    w_f = jnp.pad(bf(kv_w[:, 2 * d:]), ((0, 0), (0, LANES - n_heads)))
    fb = jnp.pad(f_bias.astype(F32), (0, LANES - n_heads)).reshape(1, LANES)
    kv = functools.partial(_shared_kv, tm=attn_tile, n_heads=n_heads)
    km, vm, cm, _, auxkm = kv(m3, w_kv, w_f, fb, jnp.zeros((1, LANES), F32),
                              rows_per_seq=n_meta, transpose_v=False)
    k, vt, _, auxq, auxk = kv(h3, w_kv, w_f, fb, cm[n_meta - 1:n_meta],
                              rows_per_seq=seq, transpose_v=True)
    pad_aux = np.zeros((LANES - n_meta, LANES), np.float32)
    pad_aux[:, AUX_K_LANE:AUX_K_LANE + n_heads] = -MASK_VALUE
    km = jnp.pad(km, ((0, LANES - n_meta), (0, 0)))
    auxkm = jnp.concatenate([auxkm, jnp.asarray(pad_aux, BF16)], axis=0)
    ones_tile = np.zeros((n_heads, BF16_SUBLANES, LANES), np.float32)
    ones_tile[:, 0, :] = 1.0
    vtm = jnp.pad(vm.T, ((0, 0), (0, LANES - n_meta))).reshape(n_heads, HEAD_DIM, LANES)
    vtm = jnp.concatenate([vtm, jnp.asarray(ones_tile, BF16)], axis=1).reshape(n_heads * V_ROWS, LANES)

    h4, _ = ffn(h3, None, 1, 1)
    q = _qproj(h4, bf(attn_w_q[0]), scale=q_scale, tm=1024)
    o = _attention(q, auxq, k, auxk, vt, km, auxkm, vtm, batch=batch, seq=seq, n_heads=n_heads,
                   tq=4 * attn_tile, tk=attn_tile)
    h5 = _oproj_ln(o, h4, bf(attn_w_o[0]), gain(1, 1), bias(1, 1), alpha=alpha, tm=1024)
    out, _ = ffn(h5, None, 2, 1, out_dtype=x.dtype)
    return out.reshape(batch, seq, d)
```

```python
import functools
import math

import jax
import jax.numpy as jnp
import numpy as np
from jax import lax
from jax.experimental import pallas as pl
from jax.experimental.pallas import tpu as pltpu

F32 = jnp.float32
BF16 = jnp.bfloat16

LN_EPS = 1e-5
MASK_VALUE = 1e30
LOG2E = math.log2(math.e)
HEAD_DIM = 128
LANES = 128
CONV_CARRY_ROWS = 8
FFN_CHUNK = 256
SUB_ROWS = 256
VMEM_LIMIT_BYTES = 56 * 1024 * 1024

N_SPLIT = 3
HEAD_GROUP = 8
AUX_Q_LANE = 0
AUX_K_LANE = N_SPLIT * HEAD_GROUP
QK_LOOKAHEAD = 2
SCORE_SLOTS = QK_LOOKAHEAD + 2
KV_UNROLL = 2
BF16_SUBLANES = 16
V_ROWS = HEAD_DIM + BF16_SUBLANES


def _ones_row_tile(width):
    row = lax.broadcasted_iota(jnp.int32, (BF16_SUBLANES, width), 0)
    return jnp.where(row == 0, 1.0, 0.0).astype(BF16)


def _params(*semantics):
    return pltpu.CompilerParams(dimension_semantics=semantics, vmem_limit_bytes=VMEM_LIMIT_BYTES)


def _layer_norm(y, gain, bias):
    mu = jnp.mean(y, axis=-1, keepdims=True)
    yc = y - mu
    var = jnp.mean(yc * yc, axis=-1, keepdims=True)
    return yc * lax.rsqrt(var + LN_EPS) * gain + bias


def _ffn_ln_kernel(*refs, alpha, n_chunks, n_side):
    if n_side:
        (x_ref, side_ref, wg_ref, wu_ref, wd_ref, gain_ref, bias_ref, o_ref, oside_ref,
         wgu_sc, wd_sc, xb_sc, gu_sc, a_sc, y_sc) = refs
    else:
        (x_ref, wg_ref, wu_ref, wd_ref, gain_ref, bias_ref, o_ref,
         wgu_sc, wd_sc, xb_sc, gu_sc, a_sc, y_sc) = refs
    i = pl.program_id(0)
    tf = wg_ref.shape[1]

    @pl.when(i < n_chunks)
    def _stage_weights():
        wgu_sc[i, :, 0:tf] = wg_ref[...].astype(BF16)
        wgu_sc[i, :, tf:2 * tf] = wu_ref[...].astype(BF16)
        wd_sc[pl.ds(pl.multiple_of(i * tf, tf), tf), :] = (0.5 * wd_ref[...]).astype(BF16)

    def ffn_rows(src_ref, dst_ref, r):
        xb_sc[0:r, :] = src_ref[...].astype(BF16)

        def gate_up(c):
            gu_sc[c % 2, 0:r, :] = jnp.dot(xb_sc[0:r, :], wgu_sc[c], preferred_element_type=F32)

        gate_up(0)
        for c in range(n_chunks):
            if c + 1 < n_chunks:
                gate_up(c + 1)
            g = gu_sc[c % 2, 0:r, 0:tf]
            u = gu_sc[c % 2, 0:r, tf:2 * tf]
            a_sc[0:r, c * tf:(c + 1) * tf] = (g * jax.nn.sigmoid(g) * u).astype(BF16)
        sub = min(SUB_ROWS, r)

        def down(rb):
            y_sc[rb % 2, 0:sub, :] = jnp.dot(a_sc[rb * sub:(rb + 1) * sub, :], wd_sc[...],
                                             preferred_element_type=F32)

        down(0)
        for rb in range(r // sub):
            if (rb + 1) * sub < r:
                down(rb + 1)
            rows = slice(rb * sub, (rb + 1) * sub)
            y = alpha * src_ref[rows, :].astype(F32) + y_sc[rb % 2, 0:sub, :]
            dst_ref[rows, :] = _layer_norm(y, gain_ref[...], bias_ref[...]).astype(dst_ref.dtype)

    if n_side:
        @pl.when(i == n_chunks)
        def _side_rows():
            ffn_rows(side_ref, oside_ref, n_side)

    @pl.when(i >= n_chunks)
    def _row_tile():
        ffn_rows(x_ref, o_ref, x_ref.shape[0])


def _ffn_ln(x, side, wg, wu, wd, layer, gain, bias, *, alpha, tm, out_dtype):
    rows, d = x.shape
    dff = wg.shape[2]
    tf = FFN_CHUNK
    n_chunks = dff // tf
    n_side = 0 if side is None else side.shape[0]
    chunk = lambda i: jnp.minimum(i, n_chunks - 1)
    tile = lambda i: (jnp.maximum(i - n_chunks, 0), 0)
    const = lambda i: (0, 0)
    in_specs = [pl.BlockSpec((tm, d), tile)]
    out_specs = [pl.BlockSpec((tm, d), tile)]
    out_shape = [jax.ShapeDtypeStruct((rows, d), out_dtype)]
    operands = [x]
    if n_side:
        in_specs.append(pl.BlockSpec((n_side, d), const))
        out_specs.append(pl.BlockSpec((n_side, d), const))
        out_shape.append(jax.ShapeDtypeStruct((n_side, d), out_dtype))
        operands.append(side)
    in_specs += [
        pl.BlockSpec((None, d, tf), lambda i: (layer, 0, chunk(i))),
        pl.BlockSpec((None, d, tf), lambda i: (layer, 0, chunk(i))),
        pl.BlockSpec((None, tf, d), lambda i: (layer, chunk(i), 0)),
        pl.BlockSpec((1, d), const),
        pl.BlockSpec((1, d), const),
    ]
    outs = pl.pallas_call(
        functools.partial(_ffn_ln_kernel, alpha=alpha, n_chunks=n_chunks, n_side=n_side),
        grid=(n_chunks + rows // tm,),
        in_specs=in_specs,
        out_specs=out_specs,
        out_shape=out_shape,
        scratch_shapes=[pltpu.VMEM((n_chunks, d, 2 * tf), BF16), pltpu.VMEM((dff, d), BF16),
                        pltpu.VMEM((tm, d), BF16), pltpu.VMEM((2, tm, 2 * tf), F32),
                        pltpu.VMEM((tm, dff), BF16), pltpu.VMEM((2, SUB_ROWS, d), F32)],
        compiler_params=_params("arbitrary"),
        name="ffn_ln",
    )(*operands, wg, wu, wd, gain, bias)
    return outs if n_side else (outs[0], None)


def _conv_ln_kernel(x_ref, win_ref, cw_ref, wout_ref, gain_ref, bias_ref, carry_ref,
                    o_ref, tail_ref, ubuf, *, alpha, tiles_per_seq):
    i = pl.program_id(0)
    tm, d = x_ref.shape
    c = CONV_CARRY_ROWS

    @pl.when(i % tiles_per_seq == 0)
    def _():
        ubuf[0:c, :] = carry_ref[...]

    x = x_ref[...]
    proj = jnp.dot(x.astype(BF16), win_ref[...], preferred_element_type=F32)
    bgate = proj[:, 0:d]
    u = proj[:, d:2 * d] * proj[:, 2 * d:3 * d]
    ubuf[c:c + tm, :] = u
    cw = cw_ref[...]
    y = (cw[2:3, :] * u
         + cw[1:2, :] * ubuf[c - 1:c - 1 + tm, :]
         + cw[0:1, :] * ubuf[c - 2:c - 2 + tm, :])
    tail = ubuf[tm:tm + c, :]
    ubuf[0:c, :] = tail
    tail_ref[...] = tail
    mix = jnp.dot((bgate * y).astype(BF16), wout_ref[...], preferred_element_type=F32)
    o_ref[...] = _layer_norm(alpha * x + mix, gain_ref[...], bias_ref[...]).astype(o_ref.dtype)


def _conv_ln(x, w_in, conv_w, w_out, gain, bias, carry, *, alpha, tm, rows_per_seq):
    rows, d = x.shape
    tm = min(tm, rows_per_seq)
    const = lambda i: (0, 0)
    return pl.pallas_call(
        functools.partial(_conv_ln_kernel, alpha=alpha, tiles_per_seq=rows_per_seq // tm),
        grid=(rows // tm,),
        in_specs=[
            pl.BlockSpec((tm, d), lambda i: (i, 0)),
            pl.BlockSpec(w_in.shape, const),
            pl.BlockSpec(conv_w.shape, const),
            pl.BlockSpec(w_out.shape, const),
            pl.BlockSpec((1, d), const),
            pl.BlockSpec((1, d), const),
            pl.BlockSpec((CONV_CARRY_ROWS, d), const),
        ],
        out_specs=[pl.BlockSpec((tm, d), lambda i: (i, 0)),
                   pl.BlockSpec((CONV_CARRY_ROWS, d), const)],
        out_shape=[jax.ShapeDtypeStruct((rows, d), F32),
                   jax.ShapeDtypeStruct((CONV_CARRY_ROWS, d), F32)],
        scratch_shapes=[pltpu.VMEM((tm + CONV_CARRY_ROWS, d), F32)],
        compiler_params=_params("arbitrary"),
        name="conv_ln",
    )(x, w_in, conv_w, w_out, gain, bias, carry)


def _kv_kernel(x_ref, wkv_ref, wf_ref, fb_ref, cin_ref, k_ref, v_ref, c_ref, auxq_ref, auxk_ref,
               carry_sc, *, tiles_per_seq, n_heads, transpose_v):
    i = pl.program_id(0)
    tm, d = x_ref.shape

    @pl.when(i % tiles_per_seq == 0)
    def _():
        carry_sc[...] = cin_ref[...]

    xb = x_ref[...].astype(BF16)
    kv = jnp.dot(xb, wkv_ref[...], preferred_element_type=F32)
    k_ref[...] = kv[:, 0:d].astype(BF16)
    if transpose_v:
        vt = kv[:, d:2 * d].T.astype(BF16)
        for hh in range(n_heads):
            v_ref[0, hh * V_ROWS:hh * V_ROWS + HEAD_DIM, :] = vt[hh * HEAD_DIM:(hh + 1) * HEAD_DIM, :]
            v_ref[0, hh * V_ROWS + HEAD_DIM:(hh + 1) * V_ROWS, :] = _ones_row_tile(tm)
    else:
        v_ref[...] = kv[:, d:2 * d].astype(BF16)

    f = jnp.dot(xb, wf_ref[...], preferred_element_type=F32) + fb_ref[...]
    log_f = jnp.minimum(f, 0.0) - jnp.log1p(jnp.exp(-jnp.abs(f)))
    row = lax.broadcasted_iota(jnp.int32, log_f.shape, 0)
    csum = log_f
    shift = 1
    while shift < tm:
        csum = csum + jnp.where(row >= shift, pltpu.roll(csum, shift, axis=0), 0.0)
        shift *= 2
    csum = csum + carry_sc[...]
    carry_sc[...] = csum[tm - 1:tm, :]
    c_ref[...] = csum

    lane = lax.broadcasted_iota(jnp.int32, csum.shape, 1)
    c2 = jnp.where(lane < n_heads, csum * LOG2E, 0.0)
    hi = c2.astype(BF16).astype(F32)
    r1 = c2 - hi
    mid = r1.astype(BF16).astype(F32)
    lo = r1 - mid
    aux = hi + pltpu.roll(mid, HEAD_GROUP, axis=1) + pltpu.roll(lo, 2 * HEAD_GROUP, axis=1)
    auxq_ref[...] = (aux.T if transpose_v else aux).astype(BF16)
    auxk_ref[...] = (-pltpu.roll(aux, AUX_K_LANE - AUX_Q_LANE, axis=1)).astype(BF16)


def _shared_kv(x, w_kv, w_f, f_bias, c_in, *, tm, rows_per_seq, n_heads, transpose_v):
    rows, d = x.shape
    tm = min(tm, rows_per_seq)
    const = lambda i: (0, 0)
    row_tile = lambda i: (i, 0)
    if transpose_v:
        v_spec = pl.BlockSpec((1, n_heads * V_ROWS, tm), lambda i: (i, 0, 0))
        v_shape = jax.ShapeDtypeStruct((rows // tm, n_heads * V_ROWS, tm), BF16)
        auxq_spec = pl.BlockSpec((LANES, tm), lambda i: (0, i))
        auxq_shape = jax.ShapeDtypeStruct((LANES, rows), BF16)
    else:
        v_spec = pl.BlockSpec((tm, d), row_tile)
        v_shape = jax.ShapeDtypeStruct((rows, d), BF16)
        auxq_spec = pl.BlockSpec((tm, LANES), row_tile)
        auxq_shape = jax.ShapeDtypeStruct((rows, LANES), BF16)
    return pl.pallas_call(
        functools.partial(_kv_kernel, tiles_per_seq=rows_per_seq // tm, n_heads=n_heads,
                          transpose_v=transpose_v),
        grid=(rows // tm,),
        in_specs=[
            pl.BlockSpec((tm, d), row_tile),
            pl.BlockSpec(w_kv.shape, const),
            pl.BlockSpec(w_f.shape, const),
            pl.BlockSpec((1, LANES), const),
            pl.BlockSpec((1, LANES), const),
        ],
        out_specs=[pl.BlockSpec((tm, d), row_tile), v_spec,
                   pl.BlockSpec((tm, LANES), row_tile),
                   auxq_spec,
                   pl.BlockSpec((tm, LANES), row_tile)],
        out_shape=[jax.ShapeDtypeStruct((rows, d), BF16), v_shape,
                   jax.ShapeDtypeStruct((rows, LANES), F32),
                   auxq_shape,
                   jax.ShapeDtypeStruct((rows, LANES), BF16)],
        scratch_shapes=[pltpu.VMEM((1, LANES), F32)],
        compiler_params=_params("arbitrary"),
        name="shared_kv",
    )(x, w_kv, w_f, f_bias, c_in)


def _sub_block_pipeline(n_sub, matmul, epilogue):
    matmul(0)
    for rb in range(n_sub):
        if rb + 1 < n_sub:
            matmul(rb + 1)
        epilogue(rb)


def _qproj_kernel(x_ref, w_ref, qt_ref, y_sc, *, scale):
    tm = x_ref.shape[0]
    sub = min(SUB_ROWS, tm)

    def matmul(rb):
        xb = x_ref[rb * sub:(rb + 1) * sub, :].astype(BF16)
        y_sc[rb % 2] = jnp.dot(xb, w_ref[...], preferred_element_type=F32)

    def epilogue(rb):
        qt_ref[:, rb * sub:(rb + 1) * sub] = (y_sc[rb % 2] * scale).T.astype(qt_ref.dtype)

    _sub_block_pipeline(tm // sub, matmul, epilogue)


def _qproj(x, w, *, scale, tm):
    rows, d = x.shape
    dq = w.shape[1]
    return pl.pallas_call(
        functools.partial(_qproj_kernel, scale=scale),
        grid=(rows // tm,),
        in_specs=[pl.BlockSpec((tm, d), lambda i: (i, 0)),
                  pl.BlockSpec(w.shape, lambda i: (0, 0))],
        out_specs=pl.BlockSpec((dq, tm), lambda i: (0, i)),
        out_shape=jax.ShapeDtypeStruct((dq, rows), BF16),
        scratch_shapes=[pltpu.VMEM((2, min(SUB_ROWS, tm), dq), F32)],
        compiler_params=_params("parallel"),
        name="qproj",
    )(x, w)


def _head_one_hot(h, base, axis):
    shape = (1, LANES) if axis == 1 else (LANES, 1)
    pos = lax.broadcasted_iota(jnp.int32, shape, axis)
    hit = (pos == base + h) | (pos == base + HEAD_GROUP + h) | (pos == base + 2 * HEAD_GROUP + h)
    return jnp.where(hit, 1.0, 0.0).astype(BF16)


def _attn_kernel(qt_ref, auxqt_ref, k_ref, auxk_ref, vt_ref, km_ref, auxkm_ref, vtm_ref, o_ref,
                 qa_sc, s_sc, m_sc, acc_sc, *, tk):
    h = pl.program_id(1)
    qi = pl.program_id(2)
    tq = qt_ref.shape[1]
    ts = tk
    n_strips = tq // ts

    q_sel = _head_one_hot(h, AUX_K_LANE, 0)
    k_sel = _head_one_hot(h, AUX_Q_LANE, 1)
    qa_sc[0:HEAD_DIM, :] = qt_ref[...]
    qa_sc[HEAD_DIM:, :] = auxqt_ref[...] + q_sel
    m_sc[...] = jnp.full(m_sc.shape, -MASK_VALUE, F32)
    acc_sc[...] = jnp.zeros(acc_sc.shape, F32)

    def strip_scores(t, ka, c, causal):
        cs = slice(c * ts, (c + 1) * ts)
        s = jnp.dot(ka, qa_sc[:, cs], preferred_element_type=F32)
        if causal:
            key = lax.broadcasted_iota(jnp.int32, s.shape, 0)
            qry = lax.broadcasted_iota(jnp.int32, s.shape, 1)
            s = jnp.where(key <= qry, s, -MASK_VALUE)
        s_sc[t % SCORE_SLOTS, 0:s.shape[0], :] = s
        return jnp.max(s, axis=0, keepdims=True)

    def strip_update(t, s_max, vt, c):
        cs = slice(c * ts, (c + 1) * ts)
        m = m_sc[:, cs]
        m_new = jnp.maximum(m, s_max)
        corr = jnp.exp2(m - m_new)
        m_sc[:, cs] = m_new
        p = jnp.exp2(s_sc[t % SCORE_SLOTS, 0:vt.shape[1], :] - m_new)
        pv = jnp.dot(vt, p.astype(BF16), preferred_element_type=F32)
        acc_sc[:, cs] = corr * acc_sc[:, cs] + pv

    def run_tasks(tasks):
        pending = [strip_scores(t, ka, c, causal)
                   for t, (ka, _, c, causal) in enumerate(tasks[:QK_LOOKAHEAD])]
        for t, (_, vt, c, _) in enumerate(tasks):
            if t + QK_LOOKAHEAD < len(tasks):
                ka_n, _, c_n, causal_n = tasks[t + QK_LOOKAHEAD]
                pending.append(strip_scores(t + QK_LOOKAHEAD, ka_n, c_n, causal_n))
            strip_update(t, pending.pop(0), vt, c)

    def key_block(j):
        off = pl.multiple_of(j * tk, tk)
        ka = jnp.concatenate([k_ref[pl.ds(off, tk), :], auxk_ref[pl.ds(off, tk), :] + k_sel], axis=1)
        return ka, vt_ref[j]

    ka_m = jnp.concatenate([km_ref[...], auxkm_ref[...] + k_sel], axis=1)
    run_tasks([(ka_m, vtm_ref[...], c, False) for c in range(n_strips)])

    def full_blocks(jj, _):
        blocks = [key_block(jj * KV_UNROLL + u) for u in range(KV_UNROLL)]
        run_tasks([(ka, vt, c, False) for ka, vt in blocks for c in range(n_strips)])
        return 0

    assert n_strips % KV_UNROLL == 0
    lax.fori_loop(0, qi * (n_strips // KV_UNROLL), full_blocks, 0)

    diag = [key_block(qi * n_strips + d) for d in range(n_strips)]
    run_tasks([(diag[d][0], diag[d][1], c, c == d) for d in range(n_strips) for c in range(d, n_strips)])

    denom = acc_sc[HEAD_DIM:HEAD_DIM + 1, :]
    o_ref[...] = (acc_sc[0:HEAD_DIM, :] / denom).T.astype(o_ref.dtype)


def _attention(qt, auxqt, k, auxk, vt, km, auxkm, vtm, *, batch, seq, n_heads, tq, tk):
    d, rows = qt.shape
    nq = seq // tq
    nk = seq // tk
    assert vt.shape == (batch * nk, n_heads * V_ROWS, tk)
    return pl.pallas_call(
        functools.partial(_attn_kernel, tk=tk),
        grid=(batch, n_heads, nq),
        in_specs=[
            pl.BlockSpec((HEAD_DIM, tq), lambda b, h, i: (h, b * nq + i)),
            pl.BlockSpec((LANES, tq), lambda b, h, i: (0, b * nq + i)),
            pl.BlockSpec((seq, HEAD_DIM), lambda b, h, i: (b, h)),
            pl.BlockSpec((seq, LANES), lambda b, h, i: (b, 0)),
            pl.BlockSpec((nk, V_ROWS, tk), lambda b, h, i: (b, h, 0)),
            pl.BlockSpec((LANES, HEAD_DIM), lambda b, h, i: (0, h)),
            pl.BlockSpec((LANES, LANES), lambda b, h, i: (0, 0)),
            pl.BlockSpec((V_ROWS, LANES), lambda b, h, i: (h, 0)),
        ],
        out_specs=pl.BlockSpec((tq, HEAD_DIM), lambda b, h, i: (b * nq + i, h)),
        out_shape=jax.ShapeDtypeStruct((rows, d), BF16),
        scratch_shapes=[pltpu.VMEM((HEAD_DIM + LANES, tq), BF16),
                        pltpu.VMEM((SCORE_SLOTS, tk, tk), F32),
                        pltpu.VMEM((1, tq), F32),
                        pltpu.VMEM((V_ROWS, tq), F32)],
        compiler_params=_params("parallel", "parallel", "arbitrary"),
        name="fox_attention",
    )(qt, auxqt, k, auxk, vt, km, auxkm, vtm)


def _oproj_ln_kernel(o_ref, x_ref, w_ref, gain_ref, bias_ref, y_ref, mix_sc, *, alpha):
    tm = x_ref.shape[0]
    sub = min(SUB_ROWS, tm)

    def matmul(rb):
        mix_sc[rb % 2] = jnp.dot(o_ref[rb * sub:(rb + 1) * sub, :], w_ref[...],
                                 preferred_element_type=F32)

    def epilogue(rb):
        rows = slice(rb * sub, (rb + 1) * sub)
        y = alpha * x_ref[rows, :] + mix_sc[rb % 2]
        y_ref[rows, :] = _layer_norm(y, gain_ref[...], bias_ref[...]).astype(y_ref.dtype)

    _sub_block_pipeline(tm // sub, matmul, epilogue)


def _oproj_ln(o, x, w, gain, bias, *, alpha, tm):
    rows, d = x.shape
    const = lambda i: (0, 0)
    return pl.pallas_call(
        functools.partial(_oproj_ln_kernel, alpha=alpha),
        grid=(rows // tm,),
        in_specs=[pl.BlockSpec((tm, d), lambda i: (i, 0)),
                  pl.BlockSpec((tm, d), lambda i: (i, 0)),
                  pl.BlockSpec(w.shape, const),
                  pl.BlockSpec((1, d), const),
                  pl.BlockSpec((1, d), const)],
        out_specs=pl.BlockSpec((tm, d), lambda i: (i, 0)),
        out_shape=jax.ShapeDtypeStruct((rows, d), F32),
        scratch_shapes=[pltpu.VMEM((2, min(SUB_ROWS, tm), d), F32)],
        compiler_params=_params("parallel"),
        name="oproj_ln",
    )(o, x, w, gain, bias)


def kernel(x, meta, ffn1_wg, ffn1_wu, ffn1_wd, ffn2_wg, ffn2_wu, ffn2_wd, ln_gain, ln_bias,
           conv_w_in, conv_w, conv_w_out, kv_w, f_bias, attn_w_q, attn_w_o):
    batch, seq, d = x.shape
    n_meta = meta.shape[0]
    depth = ffn1_wg.shape[0]
    n_heads = f_bias.shape[0]
    assert depth == 2 and conv_w_in.shape[0] == 1 and attn_w_q.shape[0] == 1
    assert d == n_heads * HEAD_DIM and n_heads == HEAD_GROUP
    assert n_meta <= LANES and n_meta % 16 == 0
    alpha = (2 * depth) ** 0.25
    q_scale = LOG2E / math.sqrt(HEAD_DIM)
    attn_tile = 512

    bf = lambda w: w.astype(BF16)
    gain = lambda l, i: ln_gain[l, i].reshape(1, d).astype(F32)
    bias = lambda l, i: ln_bias[l, i].reshape(1, d).astype(F32)
    ffn_w = {1: (ffn1_wg, ffn1_wu, ffn1_wd), 2: (ffn2_wg, ffn2_wu, ffn2_wd)}

    def ffn(h, side, which, l, out_dtype=F32):
        ln_idx = 0 if which == 1 else 2
        return _ffn_ln(h, side, *ffn_w[which], l, gain(l, ln_idx), bias(l, ln_idx),
                       alpha=alpha, tm=1024, out_dtype=out_dtype)

    xs = x.reshape(batch * seq, d)
    ms = meta.astype(x.dtype)

    w_in, w_out, cw = bf(conv_w_in[0]), bf(conv_w_out[0]), conv_w[0].astype(F32)
    conv = functools.partial(_conv_ln, alpha=alpha, tm=512)
    h1, m1 = ffn(xs, ms, 1, 0)
    zero_carry = jnp.zeros((CONV_CARRY_ROWS, d), F32)
    m2, m_tail = conv(m1, w_in, cw, w_out, gain(0, 1), bias(0, 1), zero_carry, rows_per_seq=n_meta)
    h2, _ = conv(h1, w_in, cw, w_out, gain(0, 1), bias(0, 1), m_tail, rows_per_seq=seq)
    h3, m3 = ffn(h2, m2, 2, 0)

    w_kv = bf(kv_w[:, :2 * d])
    w_f = jnp.pad(bf(kv_w[:, 2 * d:]), ((0, 0), (0, LANES - n_heads)))
    fb = jnp.pad(f_bias.astype(F32), (0, LANES - n_heads)).reshape(1, LANES)
    kv = functools.partial(_shared_kv, tm=attn_tile, n_heads=n_heads)
    km, vm, cm, _, auxkm = kv(m3, w_kv, w_f, fb, jnp.zeros((1, LANES), F32),
                              rows_per_seq=n_meta, transpose_v=False)
    k, vt, _, auxq, auxk = kv(h3, w_kv, w_f, fb, cm[n_meta - 1:n_meta],
                              rows_per_seq=seq, transpose_v=True)
    pad_aux = np.zeros((LANES - n_meta, LANES), np.float32)
    pad_aux[:, AUX_K_LANE:AUX_K_LANE + n_heads] = -MASK_VALUE
    km = jnp.pad(km, ((0, LANES - n_meta), (0, 0)))
    auxkm = jnp.concatenate([auxkm, jnp.asarray(pad_aux, BF16)], axis=0)
    ones_tile = np.zeros((n_heads, BF16_SUBLANES, LANES), np.float32)
    ones_tile[:, 0, :] = 1.0
    vtm = jnp.pad(vm.T, ((0, 0), (0, LANES - n_meta))).reshape(n_heads, HEAD_DIM, LANES)
    vtm = jnp.concatenate([vtm, jnp.asarray(ones_tile, BF16)], axis=1).reshape(n_heads * V_ROWS, LANES)

    h4, _ = ffn(h3, None, 1, 1)
    q = _qproj(h4, bf(attn_w_q[0]), scale=q_scale, tm=1024)
    o = _attention(q, auxq, k, auxk, vt, km, auxkm, vtm, batch=batch, seq=seq, n_heads=n_heads,
                   tq=4 * attn_tile, tk=attn_tile)
    h5 = _oproj_ln(o, h4, bf(attn_w_o[0]), gain(1, 1), bias(1, 1), alpha=alpha, tm=1024)
    out, _ = ffn(h5, None, 2, 1, out_dtype=x.dtype)
    return out.reshape(batch, seq, d)
```

```python
import functools
import math

import jax
import jax.numpy as jnp
import numpy as np
from jax import lax
from jax.experimental import pallas as pl
from jax.experimental.pallas import tpu as pltpu

F32 = jnp.float32
BF16 = jnp.bfloat16

LN_EPS = 1e-5
MASK_VALUE = 1e30
LOG2E = math.log2(math.e)
HEAD_DIM = 128
LANES = 128
CONV_CARRY_ROWS = 8
FFN_CHUNK = 256
SUB_ROWS = 256
VMEM_LIMIT_BYTES = 56 * 1024 * 1024

N_SPLIT = 3
HEAD_GROUP = 8
AUX_Q_LANE = 0
AUX_K_LANE = N_SPLIT * HEAD_GROUP
QK_LOOKAHEAD = 2
SCORE_SLOTS = QK_LOOKAHEAD + 2
KV_UNROLL = 2
BF16_SUBLANES = 16
V_ROWS = HEAD_DIM + BF16_SUBLANES


def _ones_row_tile(width):
    row = lax.broadcasted_iota(jnp.int32, (BF16_SUBLANES, width), 0)
    return jnp.where(row == 0, 1.0, 0.0).astype(BF16)


def _params(*semantics):
    return pltpu.CompilerParams(dimension_semantics=semantics, vmem_limit_bytes=VMEM_LIMIT_BYTES)


def _layer_norm(y, gain, bias):
    mu = jnp.mean(y, axis=-1, keepdims=True)
    yc = y - mu
    var = jnp.mean(yc * yc, axis=-1, keepdims=True)
    return yc * lax.rsqrt(var + LN_EPS) * gain + bias


def _ffn_ln_kernel(*refs, alpha, n_chunks, n_side):
    if n_side:
        (x_ref, side_ref, wg_ref, wu_ref, wd_ref, gain_ref, bias_ref, o_ref, oside_ref,
         wgu_sc, wd_sc, xb_sc, gu_sc, a_sc, y_sc) = refs
    else:
        (x_ref, wg_ref, wu_ref, wd_ref, gain_ref, bias_ref, o_ref,
         wgu_sc, wd_sc, xb_sc, gu_sc, a_sc, y_sc) = refs
    i = pl.program_id(0)
    tf = wg_ref.shape[1]

    @pl.when(i < n_chunks)
    def _stage_weights():
        wgu_sc[i, :, 0:tf] = wg_ref[...].astype(BF16)
        wgu_sc[i, :, tf:2 * tf] = wu_ref[...].astype(BF16)
        wd_sc[pl.ds(pl.multiple_of(i * tf, tf), tf), :] = (0.5 * wd_ref[...]).astype(BF16)

    def ffn_rows(src_ref, dst_ref, r):
        xb_sc[0:r, :] = src_ref[...].astype(BF16)

        def gate_up(c):
            gu_sc[c % 2, 0:r, :] = jnp.dot(xb_sc[0:r, :], wgu_sc[c], preferred_element_type=F32)

        gate_up(0)
        for c in range(n_chunks):
            if c + 1 < n_chunks:
                gate_up(c + 1)
            g = gu_sc[c % 2, 0:r, 0:tf]
            u = gu_sc[c % 2, 0:r, tf:2 * tf]
            a_sc[0:r, c * tf:(c + 1) * tf] = (g * jax.nn.sigmoid(g) * u).astype(BF16)
        sub = min(SUB_ROWS, r)

        def down(rb):
            y_sc[rb % 2, 0:sub, :] = jnp.dot(a_sc[rb * sub:(rb + 1) * sub, :], wd_sc[...],
                                             preferred_element_type=F32)

        down(0)
        for rb in range(r // sub):
            if (rb + 1) * sub < r:
                down(rb + 1)
            rows = slice(rb * sub, (rb + 1) * sub)
            y = alpha * src_ref[rows, :].astype(F32) + y_sc[rb % 2, 0:sub, :]
            dst_ref[rows, :] = _layer_norm(y, gain_ref[...], bias_ref[...]).astype(dst_ref.dtype)

    if n_side:
        @pl.when(i == n_chunks)
        def _side_rows():
            ffn_rows(side_ref, oside_ref, n_side)

    @pl.when(i >= n_chunks)
    def _row_tile():
        ffn_rows(x_ref, o_ref, x_ref.shape[0])


def _ffn_ln(x, side, wg, wu, wd, layer, gain, bias, *, alpha, tm, out_dtype):
    rows, d = x.shape
    dff = wg.shape[2]
    tf = FFN_CHUNK
    n_chunks = dff // tf
    n_side = 0 if side is None else side.shape[0]
    chunk = lambda i: jnp.minimum(i, n_chunks - 1)
    tile = lambda i: (jnp.maximum(i - n_chunks, 0), 0)
    const = lambda i: (0, 0)
    in_specs = [pl.BlockSpec((tm, d), tile)]
    out_specs = [pl.BlockSpec((tm, d), tile)]
    out_shape = [jax.ShapeDtypeStruct((rows, d), out_dtype)]
    operands = [x]
    if n_side:
        in_specs.append(pl.BlockSpec((n_side, d), const))
        out_specs.append(pl.BlockSpec((n_side, d), const))
        out_shape.append(jax.ShapeDtypeStruct((n_side, d), out_dtype))
        operands.append(side)
    in_specs += [
        pl.BlockSpec((None, d, tf), lambda i: (layer, 0, chunk(i))),
        pl.BlockSpec((None, d, tf), lambda i: (layer, 0, chunk(i))),
        pl.BlockSpec((None, tf, d), lambda i: (layer, chunk(i), 0)),
        pl.BlockSpec((1, d), const),
        pl.BlockSpec((1, d), const),
    ]
    outs = pl.pallas_call(
        functools.partial(_ffn_ln_kernel, alpha=alpha, n_chunks=n_chunks, n_side=n_side),
        grid=(n_chunks + rows // tm,),
        in_specs=in_specs,
        out_specs=out_specs,
        out_shape=out_shape,
        scratch_shapes=[pltpu.VMEM((n_chunks, d, 2 * tf), BF16), pltpu.VMEM((dff, d), BF16),
                        pltpu.VMEM((tm, d), BF16), pltpu.VMEM((2, tm, 2 * tf), F32),
                        pltpu.VMEM((tm, dff), BF16), pltpu.VMEM((2, SUB_ROWS, d), F32)],
        compiler_params=_params("arbitrary"),
        name="ffn_ln",
    )(*operands, wg, wu, wd, gain, bias)
    return outs if n_side else (outs[0], None)


def _conv_ln_kernel(x_ref, win_ref, cw_ref, wout_ref, gain_ref, bias_ref, carry_ref,
                    o_ref, tail_ref, ubuf, proj_sc, z_sc, mix_sc, *, alpha, tiles_per_seq):
    i = pl.program_id(0)
    tm, d = x_ref.shape
    c = CONV_CARRY_ROWS
    sub = min(SUB_ROWS, tm)
    n_sub = tm // sub

    @pl.when(i % tiles_per_seq == 0)
    def _():
        ubuf[0:c, :] = carry_ref[...]

    cw = cw_ref[...]

    def in_proj(rb):
        xb = x_ref[rb * sub:(rb + 1) * sub, :].astype(BF16)
        proj_sc[rb % 2] = jnp.dot(xb, win_ref[...], preferred_element_type=F32)

    def gated_conv(rb):
        slot, base = rb % 2, c + rb * sub
        u = proj_sc[slot, :, d:2 * d] * proj_sc[slot, :, 2 * d:3 * d]
        ubuf[base:base + sub, :] = u
        y = (cw[2:3, :] * u
             + cw[1:2, :] * ubuf[base - 1:base - 1 + sub, :]
             + cw[0:1, :] * ubuf[base - 2:base - 2 + sub, :])
        z_sc[slot] = (proj_sc[slot, :, 0:d] * y).astype(BF16)

    def out_proj(rb):
        mix_sc[rb % 2] = jnp.dot(z_sc[rb % 2], wout_ref[...], preferred_element_type=F32)

    def finish(rb):
        rows = slice(rb * sub, (rb + 1) * sub)
        y = alpha * x_ref[rows, :] + mix_sc[rb % 2]
        o_ref[rows, :] = _layer_norm(y, gain_ref[...], bias_ref[...]).astype(o_ref.dtype)

    in_proj(0)
    for rb in range(n_sub):
        if rb + 1 < n_sub:
            in_proj(rb + 1)
        gated_conv(rb)
        out_proj(rb)
        if rb >= 1:
            finish(rb - 1)
    finish(n_sub - 1)

    tail = ubuf[tm:tm + c, :]
    ubuf[0:c, :] = tail
    tail_ref[...] = tail


def _conv_ln(x, w_in, conv_w, w_out, gain, bias, carry, *, alpha, tm, rows_per_seq):
    rows, d = x.shape
    tm = min(tm, rows_per_seq)
    const = lambda i: (0, 0)
    return pl.pallas_call(
        functools.partial(_conv_ln_kernel, alpha=alpha, tiles_per_seq=rows_per_seq // tm),
        grid=(rows // tm,),
        in_specs=[
            pl.BlockSpec((tm, d), lambda i: (i, 0)),
            pl.BlockSpec(w_in.shape, const),
            pl.BlockSpec(conv_w.shape, const),
            pl.BlockSpec(w_out.shape, const),
            pl.BlockSpec((1, d), const),
            pl.BlockSpec((1, d), const),
            pl.BlockSpec((CONV_CARRY_ROWS, d), const),
        ],
        out_specs=[pl.BlockSpec((tm, d), lambda i: (i, 0)),
                   pl.BlockSpec((CONV_CARRY_ROWS, d), const)],
        out_shape=[jax.ShapeDtypeStruct((rows, d), F32),
                   jax.ShapeDtypeStruct((CONV_CARRY_ROWS, d), F32)],
        scratch_shapes=[pltpu.VMEM((tm + CONV_CARRY_ROWS, d), F32),
                        pltpu.VMEM((2, min(SUB_ROWS, tm), 3 * d), F32),
                        pltpu.VMEM((2, min(SUB_ROWS, tm), d), BF16),
                        pltpu.VMEM((2, min(SUB_ROWS, tm), d), F32)],
        compiler_params=_params("arbitrary"),
        name="conv_ln",
    )(x, w_in, conv_w, w_out, gain, bias, carry)


def _kv_kernel(x_ref, wkv_ref, wf_ref, fb_ref, cin_ref, k_ref, v_ref, c_ref, auxq_ref, auxk_ref,
               carry_sc, kv_sc, *, tiles_per_seq, n_heads, transpose_v):
    i = pl.program_id(0)
    tm, d = x_ref.shape
    sub = min(SUB_ROWS, tm)
    n_sub = tm // sub

    @pl.when(i % tiles_per_seq == 0)
    def _():
        carry_sc[...] = cin_ref[...]

    def kv_matmul(rb):
        xb = x_ref[rb * sub:(rb + 1) * sub, :].astype(BF16)
        kv_sc[rb % 2] = jnp.dot(xb, wkv_ref[...], preferred_element_type=F32)

    def kv_store(rb):
        rows = slice(rb * sub, (rb + 1) * sub)
        k_ref[rows, :] = kv_sc[rb % 2, :, 0:d].astype(BF16)
        if transpose_v:
            vt = kv_sc[rb % 2, :, d:2 * d].T.astype(BF16)
            for hh in range(n_heads):
                v_ref[0, hh * V_ROWS:hh * V_ROWS + HEAD_DIM, rows] = vt[hh * HEAD_DIM:(hh + 1) * HEAD_DIM, :]
        else:
            v_ref[rows, :] = kv_sc[rb % 2, :, d:2 * d].astype(BF16)

    f = jnp.dot(x_ref[...].astype(BF16), wf_ref[...], preferred_element_type=F32) + fb_ref[...]
    kv_matmul(0)
    if transpose_v:
        for hh in range(n_heads):
            v_ref[0, hh * V_ROWS + HEAD_DIM:(hh + 1) * V_ROWS, :] = _ones_row_tile(tm)
    log_f = jnp.minimum(f, 0.0) - jnp.log1p(jnp.exp(-jnp.abs(f)))
    row = lax.broadcasted_iota(jnp.int32, log_f.shape, 0)
    csum = log_f
    shift = 1
    while shift < tm:
        csum = csum + jnp.where(row >= shift, pltpu.roll(csum, shift, axis=0), 0.0)
        shift *= 2
    csum = csum + carry_sc[...]
    carry_sc[...] = csum[tm - 1:tm, :]
    c_ref[...] = csum

    lane = lax.broadcasted_iota(jnp.int32, csum.shape, 1)
    c2 = jnp.where(lane < n_heads, csum * LOG2E, 0.0)
    hi = c2.astype(BF16).astype(F32)
    r1 = c2 - hi
    mid = r1.astype(BF16).astype(F32)
    lo = r1 - mid
    aux = hi + pltpu.roll(mid, HEAD_GROUP, axis=1) + pltpu.roll(lo, 2 * HEAD_GROUP, axis=1)
    auxq_ref[...] = (aux.T if transpose_v else aux).astype(BF16)
    auxk_ref[...] = (-pltpu.roll(aux, AUX_K_LANE - AUX_Q_LANE, axis=1)).astype(BF16)

    for rb in range(n_sub):
        if rb + 1 < n_sub:
            kv_matmul(rb + 1)
        kv_store(rb)


def _shared_kv(x, w_kv, w_f, f_bias, c_in, *, tm, rows_per_seq, n_heads, transpose_v):
    rows, d = x.shape
    tm = min(tm, rows_per_seq)
    const = lambda i: (0, 0)
    row_tile = lambda i: (i, 0)
    if transpose_v:
        v_spec = pl.BlockSpec((1, n_heads * V_ROWS, tm), lambda i: (i, 0, 0))
        v_shape = jax.ShapeDtypeStruct((rows // tm, n_heads * V_ROWS, tm), BF16)
        auxq_spec = pl.BlockSpec((LANES, tm), lambda i: (0, i))
        auxq_shape = jax.ShapeDtypeStruct((LANES, rows), BF16)
    else:
        v_spec = pl.BlockSpec((tm, d), row_tile)
        v_shape = jax.ShapeDtypeStruct((rows, d), BF16)
        auxq_spec = pl.BlockSpec((tm, LANES), row_tile)
        auxq_shape = jax.ShapeDtypeStruct((rows, LANES), BF16)
    return pl.pallas_call(
        functools.partial(_kv_kernel, tiles_per_seq=rows_per_seq // tm, n_heads=n_heads,
                          transpose_v=transpose_v),
        grid=(rows // tm,),
        in_specs=[
            pl.BlockSpec((tm, d), row_tile),
            pl.BlockSpec(w_kv.shape, const),
            pl.BlockSpec(w_f.shape, const),
            pl.BlockSpec((1, LANES), const),
            pl.BlockSpec((1, LANES), const),
        ],
        out_specs=[pl.BlockSpec((tm, d), row_tile), v_spec,
                   pl.BlockSpec((tm, LANES), row_tile),
                   auxq_spec,
                   pl.BlockSpec((tm, LANES), row_tile)],
        out_shape=[jax.ShapeDtypeStruct((rows, d), BF16), v_shape,
                   jax.ShapeDtypeStruct((rows, LANES), F32),
                   auxq_shape,
                   jax.ShapeDtypeStruct((rows, LANES), BF16)],
        scratch_shapes=[pltpu.VMEM((1, LANES), F32),
                        pltpu.VMEM((2, min(SUB_ROWS, tm), 2 * d), F32)],
        compiler_params=_params("arbitrary"),
        name="shared_kv",
    )(x, w_kv, w_f, f_bias, c_in)


def _sub_block_pipeline(n_sub, matmul, epilogue):
    matmul(0)
    for rb in range(n_sub):
        if rb + 1 < n_sub:
            matmul(rb + 1)
        epilogue(rb)


def _qproj_kernel(x_ref, w_ref, qt_ref, y_sc, *, scale):
    tm = x_ref.shape[0]
    sub = min(SUB_ROWS, tm)

    def matmul(rb):
        xb = x_ref[rb * sub:(rb + 1) * sub, :].astype(BF16)
        y_sc[rb % 2] = jnp.dot(xb, w_ref[...], preferred_element_type=F32)

    def epilogue(rb):
        qt_ref[:, rb * sub:(rb + 1) * sub] = (y_sc[rb % 2] * scale).T.astype(qt_ref.dtype)

    _sub_block_pipeline(tm // sub, matmul, epilogue)


def _qproj(x, w, *, scale, tm):
    rows, d = x.shape
    dq = w.shape[1]
    return pl.pallas_call(
        functools.partial(_qproj_kernel, scale=scale),
        grid=(rows // tm,),
        in_specs=[pl.BlockSpec((tm, d), lambda i: (i, 0)),
                  pl.BlockSpec(w.shape, lambda i: (0, 0))],
        out_specs=pl.BlockSpec((dq, tm), lambda i: (0, i)),
        out_shape=jax.ShapeDtypeStruct((dq, rows), BF16),
        scratch_shapes=[pltpu.VMEM((2, min(SUB_ROWS, tm), dq), F32)],
        compiler_params=_params("parallel"),
        name="qproj",
    )(x, w)


def _head_one_hot(h, base, axis):
    shape = (1, LANES) if axis == 1 else (LANES, 1)
    pos = lax.broadcasted_iota(jnp.int32, shape, axis)
    hit = (pos == base + h) | (pos == base + HEAD_GROUP + h) | (pos == base + 2 * HEAD_GROUP + h)
    return jnp.where(hit, 1.0, 0.0).astype(BF16)


def _attn_kernel(qt_ref, auxqt_ref, k_ref, auxk_ref, vt_ref, km_ref, auxkm_ref, vtm_ref, o_ref,
                 qa_sc, s_sc, m_sc, acc_sc, *, tk):
    h = pl.program_id(1)
    qi = pl.program_id(2)
    tq = qt_ref.shape[1]
    ts = tk
    n_strips = tq // ts

    q_sel = _head_one_hot(h, AUX_K_LANE, 0)
    k_sel = _head_one_hot(h, AUX_Q_LANE, 1)
    qa_sc[0:HEAD_DIM, :] = qt_ref[...]
    qa_sc[HEAD_DIM:, :] = auxqt_ref[...] + q_sel
    m_sc[...] = jnp.full(m_sc.shape, -MASK_VALUE, F32)
    acc_sc[...] = jnp.zeros(acc_sc.shape, F32)

    def strip_scores(t, ka, c, causal):
        cs = slice(c * ts, (c + 1) * ts)
        s = jnp.dot(ka, qa_sc[:, cs], preferred_element_type=F32)
        if causal:
            key = lax.broadcasted_iota(jnp.int32, s.shape, 0)
            qry = lax.broadcasted_iota(jnp.int32, s.shape, 1)
            s = jnp.where(key <= qry, s, -MASK_VALUE)
        s_sc[t % SCORE_SLOTS, 0:s.shape[0], :] = s
        return jnp.max(s, axis=0, keepdims=True)

    def strip_update(t, s_max, vt, c):
        cs = slice(c * ts, (c + 1) * ts)
        m = m_sc[:, cs]
        m_new = jnp.maximum(m, s_max)
        corr = jnp.exp2(m - m_new)
        m_sc[:, cs] = m_new
        p = jnp.exp2(s_sc[t % SCORE_SLOTS, 0:vt.shape[1], :] - m_new)
        pv = jnp.dot(vt, p.astype(BF16), preferred_element_type=F32)
        acc_sc[:, cs] = corr * acc_sc[:, cs] + pv

    def run_tasks(tasks):
        pending = [strip_scores(t, ka, c, causal)
                   for t, (ka, _, c, causal) in enumerate(tasks[:QK_LOOKAHEAD])]
        for t, (_, vt, c, _) in enumerate(tasks):
            if t + QK_LOOKAHEAD < len(tasks):
                ka_n, _, c_n, causal_n = tasks[t + QK_LOOKAHEAD]
                pending.append(strip_scores(t + QK_LOOKAHEAD, ka_n, c_n, causal_n))
            strip_update(t, pending.pop(0), vt, c)

    def key_block(j):
        off = pl.multiple_of(j * tk, tk)
        ka = jnp.concatenate([k_ref[pl.ds(off, tk), :], auxk_ref[pl.ds(off, tk), :] + k_sel], axis=1)
        return ka, vt_ref[j]

    def full_blocks(jj, _):
        blocks = [key_block(jj * KV_UNROLL + u) for u in range(KV_UNROLL)]
        run_tasks([(ka, vt, c, False) for ka, vt in blocks for c in range(n_strips)])
        return 0

    assert n_strips % KV_UNROLL == 0
    lax.fori_loop(0, qi * (n_strips // KV_UNROLL), full_blocks, 0)

    diag = [key_block(qi * n_strips + d) for d in range(n_strips)]
    ka_m = jnp.concatenate([km_ref[...], auxkm_ref[...] + k_sel], axis=1)
    run_tasks([(diag[d][0], diag[d][1], c, c == d) for d in range(n_strips) for c in range(d, n_strips)]
              + [(ka_m, vtm_ref[...], c, False) for c in range(n_strips)])

    denom = acc_sc[HEAD_DIM:HEAD_DIM + 1, :]
    o_ref[...] = (acc_sc[0:HEAD_DIM, :] / denom).T.astype(o_ref.dtype)


def _attention(qt, auxqt, k, auxk, vt, km, auxkm, vtm, *, batch, seq, n_heads, tq, tk):
    d, rows = qt.shape
    nq = seq // tq
    nk = seq // tk
    assert vt.shape == (batch * nk, n_heads * V_ROWS, tk)
    return pl.pallas_call(
        functools.partial(_attn_kernel, tk=tk),
        grid=(batch, n_heads, nq),
        in_specs=[
            pl.BlockSpec((HEAD_DIM, tq), lambda b, h, i: (h, b * nq + i)),
            pl.BlockSpec((LANES, tq), lambda b, h, i: (0, b * nq + i)),
            pl.BlockSpec((seq, HEAD_DIM), lambda b, h, i: (b, h)),
            pl.BlockSpec((seq, LANES), lambda b, h, i: (b, 0)),
            pl.BlockSpec((nk, V_ROWS, tk), lambda b, h, i: (b, h, 0)),
            pl.BlockSpec((LANES, HEAD_DIM), lambda b, h, i: (0, h)),
            pl.BlockSpec((LANES, LANES), lambda b, h, i: (0, 0)),
            pl.BlockSpec((V_ROWS, LANES), lambda b, h, i: (h, 0)),
        ],
        out_specs=pl.BlockSpec((tq, HEAD_DIM), lambda b, h, i: (b * nq + i, h)),
        out_shape=jax.ShapeDtypeStruct((rows, d), BF16),
        scratch_shapes=[pltpu.VMEM((HEAD_DIM + LANES, tq), BF16),
                        pltpu.VMEM((SCORE_SLOTS, tk, tk), F32),
                        pltpu.VMEM((1, tq), F32),
                        pltpu.VMEM((V_ROWS, tq), F32)],
        compiler_params=_params("parallel", "parallel", "arbitrary"),
        name="fox_attention",
    )(qt, auxqt, k, auxk, vt, km, auxkm, vtm)


def _oproj_ln_kernel(o_ref, x_ref, w_ref, gain_ref, bias_ref, y_ref, mix_sc, *, alpha):
    tm = x_ref.shape[0]
    sub = min(SUB_ROWS, tm)

    def matmul(rb):
        mix_sc[rb % 2] = jnp.dot(o_ref[rb * sub:(rb + 1) * sub, :], w_ref[...],
                                 preferred_element_type=F32)

    def epilogue(rb):
        rows = slice(rb * sub, (rb + 1) * sub)
        y = alpha * x_ref[rows, :] + mix_sc[rb % 2]
        y_ref[rows, :] = _layer_norm(y, gain_ref[...], bias_ref[...]).astype(y_ref.dtype)

    _sub_block_pipeline(tm // sub, matmul, epilogue)


def _oproj_ln(o, x, w, gain, bias, *, alpha, tm):
    rows, d = x.shape
    const = lambda i: (0, 0)
    return pl.pallas_call(
        functools.partial(_oproj_ln_kernel, alpha=alpha),
        grid=(rows // tm,),
        in_specs=[pl.BlockSpec((tm, d), lambda i: (i, 0)),
                  pl.BlockSpec((tm, d), lambda i: (i, 0)),
                  pl.BlockSpec(w.shape, const),
                  pl.BlockSpec((1, d), const),
                  pl.BlockSpec((1, d), const)],
        out_specs=pl.BlockSpec((tm, d), lambda i: (i, 0)),
        out_shape=jax.ShapeDtypeStruct((rows, d), F32),
        scratch_shapes=[pltpu.VMEM((2, min(SUB_ROWS, tm), d), F32)],
        compiler_params=_params("parallel"),
        name="oproj_ln",
    )(o, x, w, gain, bias)


def kernel(x, meta, ffn1_wg, ffn1_wu, ffn1_wd, ffn2_wg, ffn2_wu, ffn2_wd, ln_gain, ln_bias,
           conv_w_in, conv_w, conv_w_out, kv_w, f_bias, attn_w_q, attn_w_o):
    batch, seq, d = x.shape
    n_meta = meta.shape[0]
    depth = ffn1_wg.shape[0]
    n_heads = f_bias.shape[0]
    assert depth == 2 and conv_w_in.shape[0] == 1 and attn_w_q.shape[0] == 1
    assert d == n_heads * HEAD_DIM and n_heads == HEAD_GROUP
    assert n_meta <= LANES and n_meta % 16 == 0
    alpha = (2 * depth) ** 0.25
    q_scale = LOG2E / math.sqrt(HEAD_DIM)
    attn_tile = 512

    bf = lambda w: w.astype(BF16)
    gain = lambda l, i: ln_gain[l, i].reshape(1, d).astype(F32)
    bias = lambda l, i: ln_bias[l, i].reshape(1, d).astype(F32)
    ffn_w = {1: (ffn1_wg, ffn1_wu, ffn1_wd), 2: (ffn2_wg, ffn2_wu, ffn2_wd)}

    def ffn(h, side, which, l, out_dtype=F32):
        ln_idx = 0 if which == 1 else 2
        return _ffn_ln(h, side, *ffn_w[which], l, gain(l, ln_idx), bias(l, ln_idx),
                       alpha=alpha, tm=1024, out_dtype=out_dtype)

    xs = x.reshape(batch * seq, d)
    ms = meta.astype(x.dtype)

    w_in, w_out, cw = bf(conv_w_in[0]), bf(conv_w_out[0]), conv_w[0].astype(F32)
    conv = functools.partial(_conv_ln, alpha=alpha, tm=1024)
    h1, m1 = ffn(xs, ms, 1, 0)
    zero_carry = jnp.zeros((CONV_CARRY_ROWS, d), F32)
    m2, m_tail = conv(m1, w_in, cw, w_out, gain(0, 1), bias(0, 1), zero_carry, rows_per_seq=n_meta)
    h2, _ = conv(h1, w_in, cw, w_out, gain(0, 1), bias(0, 1), m_tail, rows_per_seq=seq)
    h3, m3 = ffn(h2, m2, 2, 0)

    w_kv = bf(kv_w[:, :2 * d])
    w_f = jnp.pad(bf(kv_w[:, 2 * d:]), ((0, 0), (0, LANES - n_heads)))
    fb = jnp.pad(f_bias.astype(F32), (0, LANES - n_heads)).reshape(1, LANES)
    kv = functools.partial(_shared_kv, tm=attn_tile, n_heads=n_heads)
    km, vm, cm, _, auxkm = kv(m3, w_kv, w_f, fb, jnp.zeros((1, LANES), F32),
                              rows_per_seq=n_meta, transpose_v=False)
    k, vt, _, auxq, auxk = kv(h3, w_kv, w_f, fb, cm[n_meta - 1:n_meta],
                              rows_per_seq=seq, transpose_v=True)
    pad_aux = np.zeros((LANES - n_meta, LANES), np.float32)
    pad_aux[:, AUX_K_LANE:AUX_K_LANE + n_heads] = -MASK_VALUE
    km = jnp.pad(km, ((0, LANES - n_meta), (0, 0)))
    auxkm = jnp.concatenate([auxkm, jnp.asarray(pad_aux, BF16)], axis=0)
    ones_tile = np.zeros((n_heads, BF16_SUBLANES, LANES), np.float32)
    ones_tile[:, 0, :] = 1.0
    vtm = jnp.pad(vm.T, ((0, 0), (0, LANES - n_meta))).reshape(n_heads, HEAD_DIM, LANES)
    vtm = jnp.concatenate([vtm, jnp.asarray(ones_tile, BF16)], axis=1).reshape(n_heads * V_ROWS, LANES)

    h4, _ = ffn(h3, None, 1, 1)
    q = _qproj(h4, bf(attn_w_q[0]), scale=q_scale, tm=1024)
    o = _attention(q, auxq, k, auxk, vt, km, auxkm, vtm, batch=batch, seq=seq, n_heads=n_heads,
                   tq=4 * attn_tile, tk=attn_tile)
    h5 = _oproj_ln(o, h4, bf(attn_w_o[0]), gain(1, 1), bias(1, 1), alpha=alpha, tm=1024)
    out, _ = ffn(h5, None, 2, 1, out_dtype=x.dtype)
    return out.reshape(batch, seq, d)
```

```python
import functools
import math

import jax
import jax.numpy as jnp
import numpy as np
from jax import lax
from jax.experimental import pallas as pl
from jax.experimental.pallas import tpu as pltpu

F32 = jnp.float32
BF16 = jnp.bfloat16

LN_EPS = 1e-5
MASK_VALUE = 1e30
LOG2E = math.log2(math.e)
HEAD_DIM = 128
LANES = 128
CONV_CARRY_ROWS = 8
FFN_CHUNK = 256
SUB_ROWS = 256
ACT_DTYPE = BF16
VMEM_LIMIT_BYTES = 56 * 1024 * 1024

N_SPLIT = 3
HEAD_GROUP = 8
AUX_Q_LANE = 0
AUX_K_LANE = N_SPLIT * HEAD_GROUP
QK_LOOKAHEAD = 2
SCORE_SLOTS = QK_LOOKAHEAD + 2
KV_UNROLL = 2
BF16_SUBLANES = 16
V_ROWS = HEAD_DIM + BF16_SUBLANES


def _ones_row_tile(width):
    row = lax.broadcasted_iota(jnp.int32, (BF16_SUBLANES, width), 0)
    return jnp.where(row == 0, 1.0, 0.0).astype(BF16)


def _params(*semantics):
    return pltpu.CompilerParams(dimension_semantics=semantics, vmem_limit_bytes=VMEM_LIMIT_BYTES)


def _layer_norm(y, gain, bias):
    mu = jnp.mean(y, axis=-1, keepdims=True)
    yc = y - mu
    var = jnp.mean(yc * yc, axis=-1, keepdims=True)
    return yc * lax.rsqrt(var + LN_EPS) * gain + bias


def _ffn_ln_kernel(*refs, alpha, n_chunks, n_side):
    if n_side:
        (x_ref, side_ref, wg_ref, wu_ref, wd_ref, gain_ref, bias_ref, o_ref, oside_ref,
         wgu_sc, wd_sc, xb_sc, gu_sc, a_sc, y_sc) = refs
    else:
        (x_ref, wg_ref, wu_ref, wd_ref, gain_ref, bias_ref, o_ref,
         wgu_sc, wd_sc, xb_sc, gu_sc, a_sc, y_sc) = refs
    i = pl.program_id(0)
    tf = wg_ref.shape[1]

    @pl.when(i < n_chunks)
    def _stage_weights():
        wgu_sc[i, :, 0:tf] = wg_ref[...].astype(BF16)
        wgu_sc[i, :, tf:2 * tf] = wu_ref[...].astype(BF16)
        wd_sc[pl.ds(pl.multiple_of(i * tf, tf), tf), :] = (0.5 * wd_ref[...]).astype(BF16)

    def ffn_rows(src_ref, dst_ref, r):
        if src_ref.dtype == BF16:
            lhs_ref = src_ref
        else:
            xb_sc[0:r, :] = src_ref[...].astype(BF16)
            lhs_ref = xb_sc.at[0:r, :]

        def gate_up(c):
            gu_sc[c % 2, 0:r, :] = jnp.dot(lhs_ref[...], wgu_sc[c], preferred_element_type=F32)

        gate_up(0)
        for c in range(n_chunks):
            if c + 1 < n_chunks:
                gate_up(c + 1)
            g = gu_sc[c % 2, 0:r, 0:tf]
            u = gu_sc[c % 2, 0:r, tf:2 * tf]
            a_sc[0:r, c * tf:(c + 1) * tf] = (g * jax.nn.sigmoid(g) * u).astype(BF16)
        sub = min(SUB_ROWS, r)

        def down(rb):
            y_sc[rb % 2, 0:sub, :] = jnp.dot(a_sc[rb * sub:(rb + 1) * sub, :], wd_sc[...],
                                             preferred_element_type=F32)

        down(0)
        for rb in range(r // sub):
            if (rb + 1) * sub < r:
                down(rb + 1)
            rows = slice(rb * sub, (rb + 1) * sub)
            y = alpha * src_ref[rows, :].astype(F32) + y_sc[rb % 2, 0:sub, :]
            dst_ref[rows, :] = _layer_norm(y, gain_ref[...], bias_ref[...]).astype(dst_ref.dtype)

    if n_side:
        @pl.when(i == n_chunks)
        def _side_rows():
            ffn_rows(side_ref, oside_ref, n_side)

    @pl.when(i >= n_chunks)
    def _row_tile():
        ffn_rows(x_ref, o_ref, x_ref.shape[0])


def _ffn_ln(x, side, wg, wu, wd, layer, gain, bias, *, alpha, tm, out_dtype):
    rows, d = x.shape
    dff = wg.shape[2]
    tf = FFN_CHUNK
    n_chunks = dff // tf
    n_side = 0 if side is None else side.shape[0]
    chunk = lambda i: jnp.minimum(i, n_chunks - 1)
    tile = lambda i: (jnp.maximum(i - n_chunks, 0), 0)
    const = lambda i: (0, 0)
    in_specs = [pl.BlockSpec((tm, d), tile)]
    out_specs = [pl.BlockSpec((tm, d), tile)]
    out_shape = [jax.ShapeDtypeStruct((rows, d), out_dtype)]
    operands = [x]
    if n_side:
        in_specs.append(pl.BlockSpec((n_side, d), const))
        out_specs.append(pl.BlockSpec((n_side, d), const))
        out_shape.append(jax.ShapeDtypeStruct((n_side, d), out_dtype))
        operands.append(side)
    in_specs += [
        pl.BlockSpec((None, d, tf), lambda i: (layer, 0, chunk(i))),
        pl.BlockSpec((None, d, tf), lambda i: (layer, 0, chunk(i))),
        pl.BlockSpec((None, tf, d), lambda i: (layer, chunk(i), 0)),
        pl.BlockSpec((1, d), const),
        pl.BlockSpec((1, d), const),
    ]
    outs = pl.pallas_call(
        functools.partial(_ffn_ln_kernel, alpha=alpha, n_chunks=n_chunks, n_side=n_side),
        grid=(n_chunks + rows // tm,),
        in_specs=in_specs,
        out_specs=out_specs,
        out_shape=out_shape,
        scratch_shapes=[pltpu.VMEM((n_chunks, d, 2 * tf), BF16), pltpu.VMEM((dff, d), BF16),
                        pltpu.VMEM((tm, d), BF16), pltpu.VMEM((2, tm, 2 * tf), F32),
                        pltpu.VMEM((tm, dff), BF16), pltpu.VMEM((2, SUB_ROWS, d), F32)],
        compiler_params=_params("arbitrary"),
        name="ffn_ln",
    )(*operands, wg, wu, wd, gain, bias)
    return outs if n_side else (outs[0], None)


def _conv_ln_kernel(x_ref, win_ref, cw_ref, wout_ref, gain_ref, bias_ref, carry_ref,
                    o_ref, tail_ref, ubuf, proj_sc, z_sc, mix_sc, *, alpha, tiles_per_seq):
    i = pl.program_id(0)
    tm, d = x_ref.shape
    c = CONV_CARRY_ROWS
    sub = min(SUB_ROWS, tm)
    n_sub = tm // sub

    @pl.when(i % tiles_per_seq == 0)
    def _():
        ubuf[0:c, :] = carry_ref[...]

    cw = cw_ref[...]

    def in_proj(rb):
        xb = x_ref[rb * sub:(rb + 1) * sub, :].astype(BF16)
        proj_sc[rb % 2] = jnp.dot(xb, win_ref[...], preferred_element_type=F32)

    def gated_conv(rb):
        slot, base = rb % 2, c + rb * sub
        u = proj_sc[slot, :, d:2 * d] * proj_sc[slot, :, 2 * d:3 * d]
        ubuf[base:base + sub, :] = u
        y = (cw[2:3, :] * u
             + cw[1:2, :] * ubuf[base - 1:base - 1 + sub, :]
             + cw[0:1, :] * ubuf[base - 2:base - 2 + sub, :])
        z_sc[slot] = (proj_sc[slot, :, 0:d] * y).astype(BF16)

    def out_proj(rb):
        mix_sc[rb % 2] = jnp.dot(z_sc[rb % 2], wout_ref[...], preferred_element_type=F32)

    def finish(rb):
        rows = slice(rb * sub, (rb + 1) * sub)
        y = alpha * x_ref[rows, :].astype(F32) + mix_sc[rb % 2]
        o_ref[rows, :] = _layer_norm(y, gain_ref[...], bias_ref[...]).astype(o_ref.dtype)

    in_proj(0)
    for rb in range(n_sub):
        if rb + 1 < n_sub:
            in_proj(rb + 1)
        gated_conv(rb)
        out_proj(rb)
        if rb >= 1:
            finish(rb - 1)
    finish(n_sub - 1)

    tail = ubuf[tm:tm + c, :]
    ubuf[0:c, :] = tail
    tail_ref[...] = tail


def _conv_ln(x, w_in, conv_w, w_out, gain, bias, carry, *, alpha, tm, rows_per_seq):
    rows, d = x.shape
    tm = min(tm, rows_per_seq)
    const = lambda i: (0, 0)
    return pl.pallas_call(
        functools.partial(_conv_ln_kernel, alpha=alpha, tiles_per_seq=rows_per_seq // tm),
        grid=(rows // tm,),
        in_specs=[
            pl.BlockSpec((tm, d), lambda i: (i, 0)),
            pl.BlockSpec(w_in.shape, const),
            pl.BlockSpec(conv_w.shape, const),
            pl.BlockSpec(w_out.shape, const),
            pl.BlockSpec((1, d), const),
            pl.BlockSpec((1, d), const),
            pl.BlockSpec((CONV_CARRY_ROWS, d), const),
        ],
        out_specs=[pl.BlockSpec((tm, d), lambda i: (i, 0)),
                   pl.BlockSpec((CONV_CARRY_ROWS, d), const)],
        out_shape=[jax.ShapeDtypeStruct((rows, d), ACT_DTYPE),
                   jax.ShapeDtypeStruct((CONV_CARRY_ROWS, d), F32)],
        scratch_shapes=[pltpu.VMEM((tm + CONV_CARRY_ROWS, d), F32),
                        pltpu.VMEM((2, min(SUB_ROWS, tm), 3 * d), F32),
                        pltpu.VMEM((2, min(SUB_ROWS, tm), d), BF16),
                        pltpu.VMEM((2, min(SUB_ROWS, tm), d), F32)],
        compiler_params=_params("arbitrary"),
        name="conv_ln",
    )(x, w_in, conv_w, w_out, gain, bias, carry)


def _kv_kernel(x_ref, wkv_ref, wf_ref, fb_ref, cin_ref, k_ref, v_ref, c_ref, auxq_ref, auxk_ref,
               carry_sc, kv_sc, *, tiles_per_seq, n_heads, transpose_v):
    i = pl.program_id(0)
    tm, d = x_ref.shape
    sub = min(SUB_ROWS, tm)
    n_sub = tm // sub

    @pl.when(i % tiles_per_seq == 0)
    def _():
        carry_sc[...] = cin_ref[...]

    def kv_matmul(rb):
        xb = x_ref[rb * sub:(rb + 1) * sub, :].astype(BF16)
        kv_sc[rb % 2] = jnp.dot(xb, wkv_ref[...], preferred_element_type=F32)

    def kv_store(rb):
        rows = slice(rb * sub, (rb + 1) * sub)
        k_ref[rows, :] = kv_sc[rb % 2, :, 0:d].astype(BF16)
        if transpose_v:
            vt = kv_sc[rb % 2, :, d:2 * d].T.astype(BF16)
            for hh in range(n_heads):
                v_ref[0, hh * V_ROWS:hh * V_ROWS + HEAD_DIM, rows] = vt[hh * HEAD_DIM:(hh + 1) * HEAD_DIM, :]
        else:
            v_ref[rows, :] = kv_sc[rb % 2, :, d:2 * d].astype(BF16)

    f = jnp.dot(x_ref[...].astype(BF16), wf_ref[...], preferred_element_type=F32) + fb_ref[...]
    kv_matmul(0)
    if transpose_v:
        for hh in range(n_heads):
            v_ref[0, hh * V_ROWS + HEAD_DIM:(hh + 1) * V_ROWS, :] = _ones_row_tile(tm)
    log_f = jnp.minimum(f, 0.0) - jnp.log1p(jnp.exp(-jnp.abs(f)))
    row = lax.broadcasted_iota(jnp.int32, log_f.shape, 0)
    csum = log_f
    shift = 1
    while shift < tm:
        csum = csum + jnp.where(row >= shift, pltpu.roll(csum, shift, axis=0), 0.0)
        shift *= 2
    csum = csum + carry_sc[...]
    carry_sc[...] = csum[tm - 1:tm, :]
    c_ref[...] = csum

    lane = lax.broadcasted_iota(jnp.int32, csum.shape, 1)
    c2 = jnp.where(lane < n_heads, csum * LOG2E, 0.0)
    hi = c2.astype(BF16).astype(F32)
    r1 = c2 - hi
    mid = r1.astype(BF16).astype(F32)
    lo = r1 - mid
    aux = hi + pltpu.roll(mid, HEAD_GROUP, axis=1) + pltpu.roll(lo, 2 * HEAD_GROUP, axis=1)
    auxq_ref[...] = (aux.T if transpose_v else aux).astype(BF16)
    auxk_ref[...] = (-pltpu.roll(aux, AUX_K_LANE - AUX_Q_LANE, axis=1)).astype(BF16)

    for rb in range(n_sub):
        if rb + 1 < n_sub:
            kv_matmul(rb + 1)
        kv_store(rb)


def _shared_kv(x, w_kv, w_f, f_bias, c_in, *, tm, rows_per_seq, n_heads, transpose_v):
    rows, d = x.shape
    tm = min(tm, rows_per_seq)
    const = lambda i: (0, 0)
    row_tile = lambda i: (i, 0)
    if transpose_v:
        v_spec = pl.BlockSpec((1, n_heads * V_ROWS, tm), lambda i: (i, 0, 0))
        v_shape = jax.ShapeDtypeStruct((rows // tm, n_heads * V_ROWS, tm), BF16)
        auxq_spec = pl.BlockSpec((LANES, tm), lambda i: (0, i))
        auxq_shape = jax.ShapeDtypeStruct((LANES, rows), BF16)
    else:
        v_spec = pl.BlockSpec((tm, d), row_tile)
        v_shape = jax.ShapeDtypeStruct((rows, d), BF16)
        auxq_spec = pl.BlockSpec((tm, LANES), row_tile)
        auxq_shape = jax.ShapeDtypeStruct((rows, LANES), BF16)
    return pl.pallas_call(
        functools.partial(_kv_kernel, tiles_per_seq=rows_per_seq // tm, n_heads=n_heads,
                          transpose_v=transpose_v),
        grid=(rows // tm,),
        in_specs=[
            pl.BlockSpec((tm, d), row_tile),
            pl.BlockSpec(w_kv.shape, const),
            pl.BlockSpec(w_f.shape, const),
            pl.BlockSpec((1, LANES), const),
            pl.BlockSpec((1, LANES), const),
        ],
        out_specs=[pl.BlockSpec((tm, d), row_tile), v_spec,
                   pl.BlockSpec((tm, LANES), row_tile),
                   auxq_spec,
                   pl.BlockSpec((tm, LANES), row_tile)],
        out_shape=[jax.ShapeDtypeStruct((rows, d), BF16), v_shape,
                   jax.ShapeDtypeStruct((rows, LANES), F32),
                   auxq_shape,
                   jax.ShapeDtypeStruct((rows, LANES), BF16)],
        scratch_shapes=[pltpu.VMEM((1, LANES), F32),
                        pltpu.VMEM((2, min(SUB_ROWS, tm), 2 * d), F32)],
        compiler_params=_params("arbitrary"),
        name="shared_kv",
    )(x, w_kv, w_f, f_bias, c_in)


def _sub_block_pipeline(n_sub, matmul, epilogue):
    matmul(0)
    for rb in range(n_sub):
        if rb + 1 < n_sub:
            matmul(rb + 1)
        epilogue(rb)


def _qproj_kernel(x_ref, w_ref, qt_ref, y_sc, *, scale):
    tm = x_ref.shape[0]
    sub = min(SUB_ROWS, tm)

    def matmul(rb):
        xb = x_ref[rb * sub:(rb + 1) * sub, :].astype(BF16)
        y_sc[rb % 2] = jnp.dot(xb, w_ref[...], preferred_element_type=F32)

    def epilogue(rb):
        qt_ref[:, rb * sub:(rb + 1) * sub] = (y_sc[rb % 2] * scale).T.astype(qt_ref.dtype)

    _sub_block_pipeline(tm // sub, matmul, epilogue)


def _qproj(x, w, *, scale, tm):
    rows, d = x.shape
    dq = w.shape[1]
    return pl.pallas_call(
        functools.partial(_qproj_kernel, scale=scale),
        grid=(rows // tm,),
        in_specs=[pl.BlockSpec((tm, d), lambda i: (i, 0)),
                  pl.BlockSpec(w.shape, lambda i: (0, 0))],
        out_specs=pl.BlockSpec((dq, tm), lambda i: (0, i)),
        out_shape=jax.ShapeDtypeStruct((dq, rows), BF16),
        scratch_shapes=[pltpu.VMEM((2, min(SUB_ROWS, tm), dq), F32)],
        compiler_params=_params("parallel"),
        name="qproj",
    )(x, w)


def _head_one_hot(h, base, axis):
    shape = (1, LANES) if axis == 1 else (LANES, 1)
    pos = lax.broadcasted_iota(jnp.int32, shape, axis)
    hit = (pos == base + h) | (pos == base + HEAD_GROUP + h) | (pos == base + 2 * HEAD_GROUP + h)
    return jnp.where(hit, 1.0, 0.0).astype(BF16)


def _attn_kernel(qt_ref, auxqt_ref, k_ref, auxk_ref, vt_ref, km_ref, auxkm_ref, vtm_ref, o_ref,
                 qa_sc, s_sc, m_sc, acc_sc, *, tk):
    h = pl.program_id(1)
    qi = pl.program_id(2)
    tq = qt_ref.shape[1]
    ts = tk
    n_strips = tq // ts

    q_sel = _head_one_hot(h, AUX_K_LANE, 0)
    k_sel = _head_one_hot(h, AUX_Q_LANE, 1)
    qa_sc[0:HEAD_DIM, :] = qt_ref[...]
    qa_sc[HEAD_DIM:, :] = auxqt_ref[...] + q_sel
    m_sc[...] = jnp.full(m_sc.shape, -MASK_VALUE, F32)
    acc_sc[...] = jnp.zeros(acc_sc.shape, F32)

    def strip_scores(t, ka, c, causal):
        cs = slice(c * ts, (c + 1) * ts)
        s = jnp.dot(ka, qa_sc[:, cs], preferred_element_type=F32)
        if causal:
            key = lax.broadcasted_iota(jnp.int32, s.shape, 0)
            qry = lax.broadcasted_iota(jnp.int32, s.shape, 1)
            s = jnp.where(key <= qry, s, -MASK_VALUE)
        s_sc[t % SCORE_SLOTS, 0:s.shape[0], :] = s
        return jnp.max(s, axis=0, keepdims=True)

    def strip_update(t, s_max, vt, c):
        cs = slice(c * ts, (c + 1) * ts)
        m = m_sc[:, cs]
        m_new = jnp.maximum(m, s_max)
        corr = jnp.exp2(m - m_new)
        m_sc[:, cs] = m_new
        p = jnp.exp2(s_sc[t % SCORE_SLOTS, 0:vt.shape[1], :] - m_new)
        pv = jnp.dot(vt, p.astype(BF16), preferred_element_type=F32)
        acc_sc[:, cs] = corr * acc_sc[:, cs] + pv

    def run_tasks(tasks):
        pending = [strip_scores(t, ka, c, causal)
                   for t, (ka, _, c, causal) in enumerate(tasks[:QK_LOOKAHEAD])]
        for t, (_, vt, c, _) in enumerate(tasks):
            if t + QK_LOOKAHEAD < len(tasks):
                ka_n, _, c_n, causal_n = tasks[t + QK_LOOKAHEAD]
                pending.append(strip_scores(t + QK_LOOKAHEAD, ka_n, c_n, causal_n))
            strip_update(t, pending.pop(0), vt, c)

    def key_block(j):
        off = pl.multiple_of(j * tk, tk)
        ka = jnp.concatenate([k_ref[pl.ds(off, tk), :], auxk_ref[pl.ds(off, tk), :] + k_sel], axis=1)
        return ka, vt_ref[j]

    def full_blocks(jj, _):
        blocks = [key_block(jj * KV_UNROLL + u) for u in range(KV_UNROLL)]
        run_tasks([(ka, vt, c, False) for ka, vt in blocks for c in range(n_strips)])
        return 0

    assert n_strips % KV_UNROLL == 0
    lax.fori_loop(0, qi * (n_strips // KV_UNROLL), full_blocks, 0)

    diag = [key_block(qi * n_strips + d) for d in range(n_strips)]
    ka_m = jnp.concatenate([km_ref[...], auxkm_ref[...] + k_sel], axis=1)
    run_tasks([(diag[d][0], diag[d][1], c, c == d) for d in range(n_strips) for c in range(d, n_strips)]
              + [(ka_m, vtm_ref[...], c, False) for c in range(n_strips)])

    denom = acc_sc[HEAD_DIM:HEAD_DIM + 1, :]
    o_ref[...] = (acc_sc[0:HEAD_DIM, :] / denom).T.astype(o_ref.dtype)


def _attention(qt, auxqt, k, auxk, vt, km, auxkm, vtm, *, batch, seq, n_heads, tq, tk):
    d, rows = qt.shape
    nq = seq // tq
    nk = seq // tk
    assert vt.shape == (batch * nk, n_heads * V_ROWS, tk)
    return pl.pallas_call(
        functools.partial(_attn_kernel, tk=tk),
        grid=(batch, n_heads, nq),
        in_specs=[
            pl.BlockSpec((HEAD_DIM, tq), lambda b, h, i: (h, b * nq + i)),
            pl.BlockSpec((LANES, tq), lambda b, h, i: (0, b * nq + i)),
            pl.BlockSpec((seq, HEAD_DIM), lambda b, h, i: (b, h)),
            pl.BlockSpec((seq, LANES), lambda b, h, i: (b, 0)),
            pl.BlockSpec((nk, V_ROWS, tk), lambda b, h, i: (b, h, 0)),
            pl.BlockSpec((LANES, HEAD_DIM), lambda b, h, i: (0, h)),
            pl.BlockSpec((LANES, LANES), lambda b, h, i: (0, 0)),
            pl.BlockSpec((V_ROWS, LANES), lambda b, h, i: (h, 0)),
        ],
        out_specs=pl.BlockSpec((tq, HEAD_DIM), lambda b, h, i: (b * nq + i, h)),
        out_shape=jax.ShapeDtypeStruct((rows, d), BF16),
        scratch_shapes=[pltpu.VMEM((HEAD_DIM + LANES, tq), BF16),
                        pltpu.VMEM((SCORE_SLOTS, tk, tk), F32),
                        pltpu.VMEM((1, tq), F32),
                        pltpu.VMEM((V_ROWS, tq), F32)],
        compiler_params=_params("parallel", "parallel", "arbitrary"),
        name="fox_attention",
    )(qt, auxqt, k, auxk, vt, km, auxkm, vtm)


def _oproj_ln_kernel(o_ref, x_ref, w_ref, gain_ref, bias_ref, y_ref, mix_sc, *, alpha):
    tm = x_ref.shape[0]
    sub = min(SUB_ROWS, tm)

    def matmul(rb):
        mix_sc[rb % 2] = jnp.dot(o_ref[rb * sub:(rb + 1) * sub, :], w_ref[...],
                                 preferred_element_type=F32)

    def epilogue(rb):
        rows = slice(rb * sub, (rb + 1) * sub)
        y = alpha * x_ref[rows, :].astype(F32) + mix_sc[rb % 2]
        y_ref[rows, :] = _layer_norm(y, gain_ref[...], bias_ref[...]).astype(y_ref.dtype)

    _sub_block_pipeline(tm // sub, matmul, epilogue)


def _oproj_ln(o, x, w, gain, bias, *, alpha, tm):
    rows, d = x.shape
    const = lambda i: (0, 0)
    return pl.pallas_call(
        functools.partial(_oproj_ln_kernel, alpha=alpha),
        grid=(rows // tm,),
        in_specs=[pl.BlockSpec((tm, d), lambda i: (i, 0)),
                  pl.BlockSpec((tm, d), lambda i: (i, 0)),
                  pl.BlockSpec(w.shape, const),
                  pl.BlockSpec((1, d), const),
                  pl.BlockSpec((1, d), const)],
        out_specs=pl.BlockSpec((tm, d), lambda i: (i, 0)),
        out_shape=jax.ShapeDtypeStruct((rows, d), ACT_DTYPE),
        scratch_shapes=[pltpu.VMEM((2, min(SUB_ROWS, tm), d), F32)],
        compiler_params=_params("parallel"),
        name="oproj_ln",
    )(o, x, w, gain, bias)


def kernel(x, meta, ffn1_wg, ffn1_wu, ffn1_wd, ffn2_wg, ffn2_wu, ffn2_wd, ln_gain, ln_bias,
           conv_w_in, conv_w, conv_w_out, kv_w, f_bias, attn_w_q, attn_w_o):
    batch, seq, d = x.shape
    n_meta = meta.shape[0]
    depth = ffn1_wg.shape[0]
    n_heads = f_bias.shape[0]
    assert depth == 2 and conv_w_in.shape[0] == 1 and attn_w_q.shape[0] == 1
    assert d == n_heads * HEAD_DIM and n_heads == HEAD_GROUP
    assert n_meta <= LANES and n_meta % 16 == 0
    alpha = (2 * depth) ** 0.25
    q_scale = LOG2E / math.sqrt(HEAD_DIM)
    attn_tile = 512

    bf = lambda w: w.astype(BF16)
    gain = lambda l, i: ln_gain[l, i].reshape(1, d).astype(F32)
    bias = lambda l, i: ln_bias[l, i].reshape(1, d).astype(F32)
    ffn_w = {1: (ffn1_wg, ffn1_wu, ffn1_wd), 2: (ffn2_wg, ffn2_wu, ffn2_wd)}

    def ffn(h, side, which, l, out_dtype=ACT_DTYPE):
        ln_idx = 0 if which == 1 else 2
        return _ffn_ln(h, side, *ffn_w[which], l, gain(l, ln_idx), bias(l, ln_idx),
                       alpha=alpha, tm=1024, out_dtype=out_dtype)

    xs = x.reshape(batch * seq, d)
    ms = meta.astype(x.dtype)

    w_in, w_out, cw = bf(conv_w_in[0]), bf(conv_w_out[0]), conv_w[0].astype(F32)
    conv = functools.partial(_conv_ln, alpha=alpha, tm=1024)
    h1, m1 = ffn(xs, ms, 1, 0)
    zero_carry = jnp.zeros((CONV_CARRY_ROWS, d), F32)
    m2, m_tail = conv(m1, w_in, cw, w_out, gain(0, 1), bias(0, 1), zero_carry, rows_per_seq=n_meta)
    h2, _ = conv(h1, w_in, cw, w_out, gain(0, 1), bias(0, 1), m_tail, rows_per_seq=seq)
    h3, m3 = ffn(h2, m2, 2, 0)

    w_kv = bf(kv_w[:, :2 * d])
    w_f = jnp.pad(bf(kv_w[:, 2 * d:]), ((0, 0), (0, LANES - n_heads)))
    fb = jnp.pad(f_bias.astype(F32), (0, LANES - n_heads)).reshape(1, LANES)
    kv = functools.partial(_shared_kv, tm=attn_tile, n_heads=n_heads)
    km, vm, cm, _, auxkm = kv(m3, w_kv, w_f, fb, jnp.zeros((1, LANES), F32),
                              rows_per_seq=n_meta, transpose_v=False)
    k, vt, _, auxq, auxk = kv(h3, w_kv, w_f, fb, cm[n_meta - 1:n_meta],
                              rows_per_seq=seq, transpose_v=True)
    pad_aux = np.zeros((LANES - n_meta, LANES), np.float32)
    pad_aux[:, AUX_K_LANE:AUX_K_LANE + n_heads] = -MASK_VALUE
    km = jnp.pad(km, ((0, LANES - n_meta), (0, 0)))
    auxkm = jnp.concatenate([auxkm, jnp.asarray(pad_aux, BF16)], axis=0)
    ones_tile = np.zeros((n_heads, BF16_SUBLANES, LANES), np.float32)
    ones_tile[:, 0, :] = 1.0
    vtm = jnp.pad(vm.T, ((0, 0), (0, LANES - n_meta))).reshape(n_heads, HEAD_DIM, LANES)
    vtm = jnp.concatenate([vtm, jnp.asarray(ones_tile, BF16)], axis=1).reshape(n_heads * V_ROWS, LANES)

    h4, _ = ffn(h3, None, 1, 1)
    q = _qproj(h4, bf(attn_w_q[0]), scale=q_scale, tm=1024)
    o = _attention(q, auxq, k, auxk, vt, km, auxkm, vtm, batch=batch, seq=seq, n_heads=n_heads,
                   tq=4 * attn_tile, tk=attn_tile)
    h5 = _oproj_ln(o, h4, bf(attn_w_o[0]), gain(1, 1), bias(1, 1), alpha=alpha, tm=1024)
    out, _ = ffn(h5, None, 2, 1, out_dtype=x.dtype)
    return out.reshape(batch, seq, d)
```

```python
import functools
import math

import jax
import jax.numpy as jnp
import numpy as np
from jax import lax
from jax.experimental import pallas as pl
from jax.experimental.pallas import tpu as pltpu

F32 = jnp.float32
BF16 = jnp.bfloat16

LN_EPS = 1e-5
MASK_VALUE = 1e30
LOG2E = math.log2(math.e)
HEAD_DIM = 128
LANES = 128
CONV_CARRY_ROWS = 8
FFN_CHUNK = 256
SUB_ROWS = 256
ACT_DTYPE = BF16
VMEM_LIMIT_BYTES = 56 * 1024 * 1024

N_SPLIT = 3
HEAD_GROUP = 8
AUX_Q_LANE = 0
AUX_K_LANE = N_SPLIT * HEAD_GROUP
QK_LOOKAHEAD = 2
SCORE_SLOTS = QK_LOOKAHEAD + 2
KV_UNROLL = 4
BF16_SUBLANES = 16
V_ROWS = HEAD_DIM + BF16_SUBLANES


def _ones_row_tile(width):
    row = lax.broadcasted_iota(jnp.int32, (BF16_SUBLANES, width), 0)
    return jnp.where(row == 0, 1.0, 0.0).astype(BF16)


def _params(*semantics):
    return pltpu.CompilerParams(dimension_semantics=semantics, vmem_limit_bytes=VMEM_LIMIT_BYTES)


def _layer_norm(y, gain, bias):
    mu = jnp.mean(y, axis=-1, keepdims=True)
    yc = y - mu
    var = jnp.mean(yc * yc, axis=-1, keepdims=True)
    return yc * lax.rsqrt(var + LN_EPS) * gain + bias


def _ffn_ln_kernel(*refs, alpha, n_chunks, n_side):
    if n_side:
        (x_ref, side_ref, wg_ref, wu_ref, wd_ref, gain_ref, bias_ref, o_ref, oside_ref,
         wgu_sc, wd_sc, xb_sc, gu_sc, a_sc, y_sc) = refs
    else:
        (x_ref, wg_ref, wu_ref, wd_ref, gain_ref, bias_ref, o_ref,
         wgu_sc, wd_sc, xb_sc, gu_sc, a_sc, y_sc) = refs
    i = pl.program_id(0)
    tf = wg_ref.shape[1]

    @pl.when(i < n_chunks)
    def _stage_weights():
        wgu_sc[i, :, 0:tf] = wg_ref[...].astype(BF16)
        wgu_sc[i, :, tf:2 * tf] = wu_ref[...].astype(BF16)
        wd_sc[pl.ds(pl.multiple_of(i * tf, tf), tf), :] = (0.5 * wd_ref[...]).astype(BF16)

    def ffn_rows(src_ref, dst_ref, r):
        if src_ref.dtype == BF16:
            lhs_ref = src_ref
        else:
            xb_sc[0:r, :] = src_ref[...].astype(BF16)
            lhs_ref = xb_sc.at[0:r, :]

        def gate_up(c):
            gu_sc[c % 2, 0:r, :] = jnp.dot(lhs_ref[...], wgu_sc[c], preferred_element_type=F32)

        gate_up(0)
        for c in range(n_chunks):
            if c + 1 < n_chunks:
                gate_up(c + 1)
            g = gu_sc[c % 2, 0:r, 0:tf]
            u = gu_sc[c % 2, 0:r, tf:2 * tf]
            a_sc[0:r, c * tf:(c + 1) * tf] = (g * jax.nn.sigmoid(g) * u).astype(BF16)
        sub = min(SUB_ROWS, r)

        def down(rb):
            y_sc[rb % 2, 0:sub, :] = jnp.dot(a_sc[rb * sub:(rb + 1) * sub, :], wd_sc[...],
                                             preferred_element_type=F32)

        down(0)
        for rb in range(r // sub):
            if (rb + 1) * sub < r:
                down(rb + 1)
            rows = slice(rb * sub, (rb + 1) * sub)
            y = alpha * src_ref[rows, :].astype(F32) + y_sc[rb % 2, 0:sub, :]
            dst_ref[rows, :] = _layer_norm(y, gain_ref[...], bias_ref[...]).astype(dst_ref.dtype)

    if n_side:
        @pl.when(i == n_chunks)
        def _side_rows():
            ffn_rows(side_ref, oside_ref, n_side)

    @pl.when(i >= n_chunks)
    def _row_tile():
        ffn_rows(x_ref, o_ref, x_ref.shape[0])


def _ffn_ln(x, side, wg, wu, wd, layer, gain, bias, *, alpha, tm, out_dtype):
    rows, d = x.shape
    dff = wg.shape[2]
    tf = FFN_CHUNK
    n_chunks = dff // tf
    n_side = 0 if side is None else side.shape[0]
    chunk = lambda i: jnp.minimum(i, n_chunks - 1)
    tile = lambda i: (jnp.maximum(i - n_chunks, 0), 0)
    const = lambda i: (0, 0)
    in_specs = [pl.BlockSpec((tm, d), tile)]
    out_specs = [pl.BlockSpec((tm, d), tile)]
    out_shape = [jax.ShapeDtypeStruct((rows, d), out_dtype)]
    operands = [x]
    if n_side:
        in_specs.append(pl.BlockSpec((n_side, d), const))
        out_specs.append(pl.BlockSpec((n_side, d), const))
        out_shape.append(jax.ShapeDtypeStruct((n_side, d), out_dtype))
        operands.append(side)
    in_specs += [
        pl.BlockSpec((None, d, tf), lambda i: (layer, 0, chunk(i))),
        pl.BlockSpec((None, d, tf), lambda i: (layer, 0, chunk(i))),
        pl.BlockSpec((None, tf, d), lambda i: (layer, chunk(i), 0)),
        pl.BlockSpec((1, d), const),
        pl.BlockSpec((1, d), const),
    ]
    outs = pl.pallas_call(
        functools.partial(_ffn_ln_kernel, alpha=alpha, n_chunks=n_chunks, n_side=n_side),
        grid=(n_chunks + rows // tm,),
        in_specs=in_specs,
        out_specs=out_specs,
        out_shape=out_shape,
        scratch_shapes=[pltpu.VMEM((n_chunks, d, 2 * tf), BF16), pltpu.VMEM((dff, d), BF16),
                        pltpu.VMEM((tm, d), BF16), pltpu.VMEM((2, tm, 2 * tf), F32),
                        pltpu.VMEM((tm, dff), BF16), pltpu.VMEM((2, SUB_ROWS, d), F32)],
        compiler_params=_params("arbitrary"),
        name="ffn_ln",
    )(*operands, wg, wu, wd, gain, bias)
    return outs if n_side else (outs[0], None)


def _conv_ln_kernel(x_ref, win_ref, cw_ref, wout_ref, gain_ref, bias_ref, carry_ref,
                    o_ref, tail_ref, ubuf, proj_sc, z_sc, mix_sc, *, alpha, tiles_per_seq):
    i = pl.program_id(0)
    tm, d = x_ref.shape
    c = CONV_CARRY_ROWS
    sub = min(SUB_ROWS, tm)
    n_sub = tm // sub

    @pl.when(i % tiles_per_seq == 0)
    def _():
        ubuf[0:c, :] = carry_ref[...]

    cw = cw_ref[...]

    def in_proj(rb):
        xb = x_ref[rb * sub:(rb + 1) * sub, :].astype(BF16)
        proj_sc[rb % 2] = jnp.dot(xb, win_ref[...], preferred_element_type=F32)

    def gated_conv(rb):
        slot, base = rb % 2, c + rb * sub
        u = proj_sc[slot, :, d:2 * d] * proj_sc[slot, :, 2 * d:3 * d]
        ubuf[base:base + sub, :] = u
        y = (cw[2:3, :] * u
             + cw[1:2, :] * ubuf[base - 1:base - 1 + sub, :]
             + cw[0:1, :] * ubuf[base - 2:base - 2 + sub, :])
        z_sc[slot] = (proj_sc[slot, :, 0:d] * y).astype(BF16)

    def out_proj(rb):
        mix_sc[rb % 2] = jnp.dot(z_sc[rb % 2], wout_ref[...], preferred_element_type=F32)

    def finish(rb):
        rows = slice(rb * sub, (rb + 1) * sub)
        y = alpha * x_ref[rows, :].astype(F32) + mix_sc[rb % 2]
        o_ref[rows, :] = _layer_norm(y, gain_ref[...], bias_ref[...]).astype(o_ref.dtype)

    in_proj(0)
    for rb in range(n_sub):
        if rb + 1 < n_sub:
            in_proj(rb + 1)
        gated_conv(rb)
        out_proj(rb)
        if rb >= 1:
            finish(rb - 1)
    finish(n_sub - 1)

    tail = ubuf[tm:tm + c, :]
    ubuf[0:c, :] = tail
    tail_ref[...] = tail


def _conv_ln(x, w_in, conv_w, w_out, gain, bias, carry, *, alpha, tm, rows_per_seq):
    rows, d = x.shape
    tm = min(tm, rows_per_seq)
    const = lambda i: (0, 0)
    return pl.pallas_call(
        functools.partial(_conv_ln_kernel, alpha=alpha, tiles_per_seq=rows_per_seq // tm),
        grid=(rows // tm,),
        in_specs=[
            pl.BlockSpec((tm, d), lambda i: (i, 0)),
            pl.BlockSpec(w_in.shape, const),
            pl.BlockSpec(conv_w.shape, const),
            pl.BlockSpec(w_out.shape, const),
            pl.BlockSpec((1, d), const),
            pl.BlockSpec((1, d), const),
            pl.BlockSpec((CONV_CARRY_ROWS, d), const),
        ],
        out_specs=[pl.BlockSpec((tm, d), lambda i: (i, 0)),
                   pl.BlockSpec((CONV_CARRY_ROWS, d), const)],
        out_shape=[jax.ShapeDtypeStruct((rows, d), ACT_DTYPE),
                   jax.ShapeDtypeStruct((CONV_CARRY_ROWS, d), F32)],
        scratch_shapes=[pltpu.VMEM((tm + CONV_CARRY_ROWS, d), F32),
                        pltpu.VMEM((2, min(SUB_ROWS, tm), 3 * d), F32),
                        pltpu.VMEM((2, min(SUB_ROWS, tm), d), BF16),
                        pltpu.VMEM((2, min(SUB_ROWS, tm), d), F32)],
        compiler_params=_params("arbitrary"),
        name="conv_ln",
    )(x, w_in, conv_w, w_out, gain, bias, carry)


def _kv_kernel(x_ref, wkv_ref, wf_ref, fb_ref, cin_ref, k_ref, v_ref, c_ref, auxq_ref, auxk_ref,
               carry_sc, kv_sc, *, tiles_per_seq, n_heads, transpose_v):
    i = pl.program_id(0)
    tm, d = x_ref.shape
    sub = min(SUB_ROWS, tm)
    n_sub = tm // sub

    @pl.when(i % tiles_per_seq == 0)
    def _():
        carry_sc[...] = cin_ref[...]

    def kv_matmul(rb):
        xb = x_ref[rb * sub:(rb + 1) * sub, :].astype(BF16)
        kv_sc[rb % 2] = jnp.dot(xb, wkv_ref[...], preferred_element_type=F32)

    def kv_store(rb):
        rows = slice(rb * sub, (rb + 1) * sub)
        k_ref[rows, :] = kv_sc[rb % 2, :, 0:d].astype(BF16)
        if transpose_v:
            vt = kv_sc[rb % 2, :, d:2 * d].T.astype(BF16)
            for hh in range(n_heads):
                v_ref[0, hh * V_ROWS:hh * V_ROWS + HEAD_DIM, rows] = vt[hh * HEAD_DIM:(hh + 1) * HEAD_DIM, :]
        else:
            v_ref[rows, :] = kv_sc[rb % 2, :, d:2 * d].astype(BF16)

    f = jnp.dot(x_ref[...].astype(BF16), wf_ref[...], preferred_element_type=F32) + fb_ref[...]
    kv_matmul(0)
    if transpose_v:
        for hh in range(n_heads):
            v_ref[0, hh * V_ROWS + HEAD_DIM:(hh + 1) * V_ROWS, :] = _ones_row_tile(tm)
    log_f = jnp.minimum(f, 0.0) - jnp.log1p(jnp.exp(-jnp.abs(f)))
    row = lax.broadcasted_iota(jnp.int32, log_f.shape, 0)
    csum = log_f
    shift = 1
    while shift < tm:
        csum = csum + jnp.where(row >= shift, pltpu.roll(csum, shift, axis=0), 0.0)
        shift *= 2
    csum = csum + carry_sc[...]
    carry_sc[...] = csum[tm - 1:tm, :]
    c_ref[...] = csum

    lane = lax.broadcasted_iota(jnp.int32, csum.shape, 1)
    c2 = jnp.where(lane < n_heads, csum * LOG2E, 0.0)
    hi = c2.astype(BF16).astype(F32)
    r1 = c2 - hi
    mid = r1.astype(BF16).astype(F32)
    lo = r1 - mid
    aux = hi + pltpu.roll(mid, HEAD_GROUP, axis=1) + pltpu.roll(lo, 2 * HEAD_GROUP, axis=1)
    auxq_ref[...] = (aux.T if transpose_v else aux).astype(BF16)
    auxk_ref[...] = (-pltpu.roll(aux, AUX_K_LANE - AUX_Q_LANE, axis=1)).astype(BF16)

    for rb in range(n_sub):
        if rb + 1 < n_sub:
            kv_matmul(rb + 1)
        kv_store(rb)


def _shared_kv(x, w_kv, w_f, f_bias, c_in, *, tm, rows_per_seq, n_heads, transpose_v):
    rows, d = x.shape
    tm = min(tm, rows_per_seq)
    const = lambda i: (0, 0)
    row_tile = lambda i: (i, 0)
    if transpose_v:
        v_spec = pl.BlockSpec((1, n_heads * V_ROWS, tm), lambda i: (i, 0, 0))
        v_shape = jax.ShapeDtypeStruct((rows // tm, n_heads * V_ROWS, tm), BF16)
        auxq_spec = pl.BlockSpec((LANES, tm), lambda i: (0, i))
        auxq_shape = jax.ShapeDtypeStruct((LANES, rows), BF16)
    else:
        v_spec = pl.BlockSpec((tm, d), row_tile)
        v_shape = jax.ShapeDtypeStruct((rows, d), BF16)
        auxq_spec = pl.BlockSpec((tm, LANES), row_tile)
        auxq_shape = jax.ShapeDtypeStruct((rows, LANES), BF16)
    return pl.pallas_call(
        functools.partial(_kv_kernel, tiles_per_seq=rows_per_seq // tm, n_heads=n_heads,
                          transpose_v=transpose_v),
        grid=(rows // tm,),
        in_specs=[
            pl.BlockSpec((tm, d), row_tile),
            pl.BlockSpec(w_kv.shape, const),
            pl.BlockSpec(w_f.shape, const),
            pl.BlockSpec((1, LANES), const),
            pl.BlockSpec((1, LANES), const),
        ],
        out_specs=[pl.BlockSpec((tm, d), row_tile), v_spec,
                   pl.BlockSpec((tm, LANES), row_tile),
                   auxq_spec,
                   pl.BlockSpec((tm, LANES), row_tile)],
        out_shape=[jax.ShapeDtypeStruct((rows, d), BF16), v_shape,
                   jax.ShapeDtypeStruct((rows, LANES), F32),
                   auxq_shape,
                   jax.ShapeDtypeStruct((rows, LANES), BF16)],
        scratch_shapes=[pltpu.VMEM((1, LANES), F32),
                        pltpu.VMEM((2, min(SUB_ROWS, tm), 2 * d), F32)],
        compiler_params=_params("arbitrary"),
        name="shared_kv",
    )(x, w_kv, w_f, f_bias, c_in)


def _sub_block_pipeline(n_sub, matmul, epilogue):
    matmul(0)
    for rb in range(n_sub):
        if rb + 1 < n_sub:
            matmul(rb + 1)
        epilogue(rb)


def _qproj_kernel(x_ref, w_ref, qt_ref, y_sc, *, scale):
    tm = x_ref.shape[0]
    sub = min(SUB_ROWS, tm)

    def matmul(rb):
        xb = x_ref[rb * sub:(rb + 1) * sub, :].astype(BF16)
        y_sc[rb % 2] = jnp.dot(xb, w_ref[...], preferred_element_type=F32)

    def epilogue(rb):
        qt_ref[:, rb * sub:(rb + 1) * sub] = (y_sc[rb % 2] * scale).T.astype(qt_ref.dtype)

    _sub_block_pipeline(tm // sub, matmul, epilogue)


def _qproj(x, w, *, scale, tm):
    rows, d = x.shape
    dq = w.shape[1]
    return pl.pallas_call(
        functools.partial(_qproj_kernel, scale=scale),
        grid=(rows // tm,),
        in_specs=[pl.BlockSpec((tm, d), lambda i: (i, 0)),
                  pl.BlockSpec(w.shape, lambda i: (0, 0))],
        out_specs=pl.BlockSpec((dq, tm), lambda i: (0, i)),
        out_shape=jax.ShapeDtypeStruct((dq, rows), BF16),
        scratch_shapes=[pltpu.VMEM((2, min(SUB_ROWS, tm), dq), F32)],
        compiler_params=_params("parallel"),
        name="qproj",
    )(x, w)


def _head_one_hot(h, base, axis):
    shape = (1, LANES) if axis == 1 else (LANES, 1)
    pos = lax.broadcasted_iota(jnp.int32, shape, axis)
    hit = (pos == base + h) | (pos == base + HEAD_GROUP + h) | (pos == base + 2 * HEAD_GROUP + h)
    return jnp.where(hit, 1.0, 0.0).astype(BF16)


def _attn_kernel(qt_ref, auxqt_ref, k_ref, auxk_ref, vt_ref, km_ref, auxkm_ref, vtm_ref, o_ref,
                 qa_sc, s_sc, m_sc, acc_sc, *, tk):
    h = pl.program_id(1)
    qi = pl.program_id(2)
    tq = qt_ref.shape[1]
    ts = tk
    n_strips = tq // ts

    q_sel = _head_one_hot(h, AUX_K_LANE, 0)
    k_sel = _head_one_hot(h, AUX_Q_LANE, 1)
    qa_sc[0:HEAD_DIM, :] = qt_ref[...]
    qa_sc[HEAD_DIM:, :] = auxqt_ref[...] + q_sel
    m_sc[...] = jnp.full(m_sc.shape, -MASK_VALUE, F32)
    acc_sc[...] = jnp.zeros(acc_sc.shape, F32)

    def strip_scores(t, ka, c, causal):
        cs = slice(c * ts, (c + 1) * ts)
        s = jnp.dot(ka, qa_sc[:, cs], preferred_element_type=F32)
        if causal:
            key = lax.broadcasted_iota(jnp.int32, s.shape, 0)
            qry = lax.broadcasted_iota(jnp.int32, s.shape, 1)
            s = jnp.where(key <= qry, s, -MASK_VALUE)
        s_sc[t % SCORE_SLOTS, 0:s.shape[0], :] = s
        return jnp.max(s, axis=0, keepdims=True)

    def strip_update(t, s_max, vt, c):
        cs = slice(c * ts, (c + 1) * ts)
        m = m_sc[:, cs]
        m_new = jnp.maximum(m, s_max)
        corr = jnp.exp2(m - m_new)
        m_sc[:, cs] = m_new
        p = jnp.exp2(s_sc[t % SCORE_SLOTS, 0:vt.shape[1], :] - m_new)
        pv = jnp.dot(vt, p.astype(BF16), preferred_element_type=F32)
        acc_sc[:, cs] = corr * acc_sc[:, cs] + pv

    def run_tasks(tasks):
        pending = [strip_scores(t, ka, c, causal)
                   for t, (ka, _, c, causal) in enumerate(tasks[:QK_LOOKAHEAD])]
        for t, (_, vt, c, _) in enumerate(tasks):
            if t + QK_LOOKAHEAD < len(tasks):
                ka_n, _, c_n, causal_n = tasks[t + QK_LOOKAHEAD]
                pending.append(strip_scores(t + QK_LOOKAHEAD, ka_n, c_n, causal_n))
            strip_update(t, pending.pop(0), vt, c)

    def key_block(j):
        off = pl.multiple_of(j * tk, tk)
        ka = jnp.concatenate([k_ref[pl.ds(off, tk), :], auxk_ref[pl.ds(off, tk), :] + k_sel], axis=1)
        return ka, vt_ref[j]

    def full_blocks(jj, _):
        blocks = [key_block(jj * KV_UNROLL + u) for u in range(KV_UNROLL)]
        run_tasks([(ka, vt, c, False) for ka, vt in blocks for c in range(n_strips)])
        return 0

    assert n_strips % KV_UNROLL == 0
    lax.fori_loop(0, qi * (n_strips // KV_UNROLL), full_blocks, 0)

    diag = [key_block(qi * n_strips + d) for d in range(n_strips)]
    ka_m = jnp.concatenate([km_ref[...], auxkm_ref[...] + k_sel], axis=1)
    run_tasks([(diag[d][0], diag[d][1], c, c == d) for d in range(n_strips) for c in range(d, n_strips)]
              + [(ka_m, vtm_ref[...], c, False) for c in range(n_strips)])

    denom = acc_sc[HEAD_DIM:HEAD_DIM + 1, :]
    o_ref[...] = (acc_sc[0:HEAD_DIM, :] / denom).T.astype(o_ref.dtype)


def _attention(qt, auxqt, k, auxk, vt, km, auxkm, vtm, *, batch, seq, n_heads, tq, tk):
    d, rows = qt.shape
    nq = seq // tq
    nk = seq // tk
    assert vt.shape == (batch * nk, n_heads * V_ROWS, tk)
    return pl.pallas_call(
        functools.partial(_attn_kernel, tk=tk),
        grid=(batch, n_heads, nq),
        in_specs=[
            pl.BlockSpec((HEAD_DIM, tq), lambda b, h, i: (h, b * nq + i)),
            pl.BlockSpec((LANES, tq), lambda b, h, i: (0, b * nq + i)),
            pl.BlockSpec((seq, HEAD_DIM), lambda b, h, i: (b, h)),
            pl.BlockSpec((seq, LANES), lambda b, h, i: (b, 0)),
            pl.BlockSpec((nk, V_ROWS, tk), lambda b, h, i: (b, h, 0)),
            pl.BlockSpec((LANES, HEAD_DIM), lambda b, h, i: (0, h)),
            pl.BlockSpec((LANES, LANES), lambda b, h, i: (0, 0)),
            pl.BlockSpec((V_ROWS, LANES), lambda b, h, i: (h, 0)),
        ],
        out_specs=pl.BlockSpec((tq, HEAD_DIM), lambda b, h, i: (b * nq + i, h)),
        out_shape=jax.ShapeDtypeStruct((rows, d), BF16),
        scratch_shapes=[pltpu.VMEM((HEAD_DIM + LANES, tq), BF16),
                        pltpu.VMEM((SCORE_SLOTS, tk, tk), F32),
                        pltpu.VMEM((1, tq), F32),
                        pltpu.VMEM((V_ROWS, tq), F32)],
        compiler_params=_params("parallel", "parallel", "arbitrary"),
        name="fox_attention",
    )(qt, auxqt, k, auxk, vt, km, auxkm, vtm)


def _oproj_ln_kernel(o_ref, x_ref, w_ref, gain_ref, bias_ref, y_ref, mix_sc, *, alpha):
    tm = x_ref.shape[0]
    sub = min(SUB_ROWS, tm)

    def matmul(rb):
        mix_sc[rb % 2] = jnp.dot(o_ref[rb * sub:(rb + 1) * sub, :], w_ref[...],
                                 preferred_element_type=F32)

    def epilogue(rb):
        rows = slice(rb * sub, (rb + 1) * sub)
        y = alpha * x_ref[rows, :].astype(F32) + mix_sc[rb % 2]
        y_ref[rows, :] = _layer_norm(y, gain_ref[...], bias_ref[...]).astype(y_ref.dtype)

    _sub_block_pipeline(tm // sub, matmul, epilogue)


def _oproj_ln(o, x, w, gain, bias, *, alpha, tm):
    rows, d = x.shape
    const = lambda i: (0, 0)
    return pl.pallas_call(
        functools.partial(_oproj_ln_kernel, alpha=alpha),
        grid=(rows // tm,),
        in_specs=[pl.BlockSpec((tm, d), lambda i: (i, 0)),
                  pl.BlockSpec((tm, d), lambda i: (i, 0)),
                  pl.BlockSpec(w.shape, const),
                  pl.BlockSpec((1, d), const),
                  pl.BlockSpec((1, d), const)],
        out_specs=pl.BlockSpec((tm, d), lambda i: (i, 0)),
        out_shape=jax.ShapeDtypeStruct((rows, d), ACT_DTYPE),
        scratch_shapes=[pltpu.VMEM((2, min(SUB_ROWS, tm), d), F32)],
        compiler_params=_params("parallel"),
        name="oproj_ln",
    )(o, x, w, gain, bias)


def kernel(x, meta, ffn1_wg, ffn1_wu, ffn1_wd, ffn2_wg, ffn2_wu, ffn2_wd, ln_gain, ln_bias,
           conv_w_in, conv_w, conv_w_out, kv_w, f_bias, attn_w_q, attn_w_o):
    batch, seq, d = x.shape
    n_meta = meta.shape[0]
    depth = ffn1_wg.shape[0]
    n_heads = f_bias.shape[0]
    assert depth == 2 and conv_w_in.shape[0] == 1 and attn_w_q.shape[0] == 1
    assert d == n_heads * HEAD_DIM and n_heads == HEAD_GROUP
    assert n_meta <= LANES and n_meta % 16 == 0
    alpha = (2 * depth) ** 0.25
    q_scale = LOG2E / math.sqrt(HEAD_DIM)
    attn_tile = 512

    bf = lambda w: w.astype(BF16)
    gain = lambda l, i: ln_gain[l, i].reshape(1, d).astype(F32)
    bias = lambda l, i: ln_bias[l, i].reshape(1, d).astype(F32)
    ffn_w = {1: (ffn1_wg, ffn1_wu, ffn1_wd), 2: (ffn2_wg, ffn2_wu, ffn2_wd)}

    def ffn(h, side, which, l, out_dtype=ACT_DTYPE):
        ln_idx = 0 if which == 1 else 2
        return _ffn_ln(h, side, *ffn_w[which], l, gain(l, ln_idx), bias(l, ln_idx),
                       alpha=alpha, tm=1024, out_dtype=out_dtype)

    xs = x.reshape(batch * seq, d)
    ms = meta.astype(x.dtype)

    w_in, w_out, cw = bf(conv_w_in[0]), bf(conv_w_out[0]), conv_w[0].astype(F32)
    conv = functools.partial(_conv_ln, alpha=alpha, tm=1024)
    h1, m1 = ffn(xs, ms, 1, 0)
    zero_carry = jnp.zeros((CONV_CARRY_ROWS, d), F32)
    m2, m_tail = conv(m1, w_in, cw, w_out, gain(0, 1), bias(0, 1), zero_carry, rows_per_seq=n_meta)
    h2, _ = conv(h1, w_in, cw, w_out, gain(0, 1), bias(0, 1), m_tail, rows_per_seq=seq)
    h3, m3 = ffn(h2, m2, 2, 0)

    w_kv = bf(kv_w[:, :2 * d])
    w_f = jnp.pad(bf(kv_w[:, 2 * d:]), ((0, 0), (0, LANES - n_heads)))
    fb = jnp.pad(f_bias.astype(F32), (0, LANES - n_heads)).reshape(1, LANES)
    kv = functools.partial(_shared_kv, tm=attn_tile, n_heads=n_heads)
    km, vm, cm, _, auxkm = kv(m3, w_kv, w_f, fb, jnp.zeros((1, LANES), F32),
                              rows_per_seq=n_meta, transpose_v=False)
    k, vt, _, auxq, auxk = kv(h3, w_kv, w_f, fb, cm[n_meta - 1:n_meta],
                              rows_per_seq=seq, transpose_v=True)
    pad_aux = np.zeros((LANES - n_meta, LANES), np.float32)
    pad_aux[:, AUX_K_LANE:AUX_K_LANE + n_heads] = -MASK_VALUE
    km = jnp.pad(km, ((0, LANES - n_meta), (0, 0)))
    auxkm = jnp.concatenate([auxkm, jnp.asarray(pad_aux, BF16)], axis=0)
    ones_tile = np.zeros((n_heads, BF16_SUBLANES, LANES), np.float32)
    ones_tile[:, 0, :] = 1.0
    vtm = jnp.pad(vm.T, ((0, 0), (0, LANES - n_meta))).reshape(n_heads, HEAD_DIM, LANES)
    vtm = jnp.concatenate([vtm, jnp.asarray(ones_tile, BF16)], axis=1).reshape(n_heads * V_ROWS, LANES)

    h4, _ = ffn(h3, None, 1, 1)
    q = _qproj(h4, bf(attn_w_q[0]), scale=q_scale, tm=1024)
    o = _attention(q, auxq, k, auxk, vt, km, auxkm, vtm, batch=batch, seq=seq, n_heads=n_heads,
                   tq=4 * attn_tile, tk=attn_tile)
    h5 = _oproj_ln(o, h4, bf(attn_w_o[0]), gain(1, 1), bias(1, 1), alpha=alpha, tm=1024)
    out, _ = ffn(h5, None, 2, 1, out_dtype=x.dtype)
    return out.reshape(batch, seq, d)
```

```python
import functools
import math

import jax
import jax.numpy as jnp
import numpy as np
from jax import lax
from jax.experimental import pallas as pl
from jax.experimental.pallas import tpu as pltpu

F32 = jnp.float32
BF16 = jnp.bfloat16

LN_EPS = 1e-5
MASK_VALUE = 1e30
LOG2E = math.log2(math.e)
HEAD_DIM = 128
LANES = 128
CONV_CARRY_ROWS = 8
FFN_CHUNK = 256
SUB_ROWS = 256
ACT_DTYPE = BF16
VMEM_LIMIT_BYTES = 56 * 1024 * 1024

N_SPLIT = 3
HEAD_GROUP = 8
AUX_Q_LANE = 0
AUX_K_LANE = N_SPLIT * HEAD_GROUP
QK_LOOKAHEAD = 2
SCORE_SLOTS = QK_LOOKAHEAD + 2
KV_UNROLL = 4
BF16_SUBLANES = 16
V_ROWS = HEAD_DIM + BF16_SUBLANES


def _ones_row_tile(width):
    row = lax.broadcasted_iota(jnp.int32, (BF16_SUBLANES, width), 0)
    return jnp.where(row == 0, 1.0, 0.0).astype(BF16)


def _params(*semantics):
    return pltpu.CompilerParams(dimension_semantics=semantics, vmem_limit_bytes=VMEM_LIMIT_BYTES)


def _layer_norm(y, gain, bias):
    mu = jnp.mean(y, axis=-1, keepdims=True)
    yc = y - mu
    var = jnp.mean(yc * yc, axis=-1, keepdims=True)
    return yc * lax.rsqrt(var + LN_EPS) * gain + bias


def _ffn_ln_kernel(*refs, alpha, n_chunks, n_side):
    if n_side:
        (x_ref, side_ref, wg_ref, wu_ref, wd_ref, gain_ref, bias_ref, o_ref, oside_ref,
         wgu_sc, wd_sc, xb_sc, gu_sc, a_sc, y_sc) = refs
    else:
        (x_ref, wg_ref, wu_ref, wd_ref, gain_ref, bias_ref, o_ref,
         wgu_sc, wd_sc, xb_sc, gu_sc, a_sc, y_sc) = refs
    i = pl.program_id(0)
    tf = wg_ref.shape[1]

    @pl.when(i < n_chunks)
    def _stage_weights():
        wgu_sc[i, :, 0:tf] = wg_ref[...].astype(BF16)
        wgu_sc[i, :, tf:2 * tf] = wu_ref[...].astype(BF16)
        wd_sc[pl.ds(pl.multiple_of(i * tf, tf), tf), :] = (0.5 * wd_ref[...]).astype(BF16)

    def ffn_rows(src_ref, dst_ref, r):
        if src_ref.dtype == BF16:
            lhs_ref = src_ref
        else:
            xb_sc[0:r, :] = src_ref[...].astype(BF16)
            lhs_ref = xb_sc.at[0:r, :]

        def gate_up(c):
            gu_sc[c % 2, 0:r, :] = jnp.dot(lhs_ref[...], wgu_sc[c], preferred_element_type=F32)

        gate_up(0)
        for c in range(n_chunks):
            if c + 1 < n_chunks:
                gate_up(c + 1)
            g = gu_sc[c % 2, 0:r, 0:tf]
            u = gu_sc[c % 2, 0:r, tf:2 * tf]
            a_sc[0:r, c * tf:(c + 1) * tf] = (g * jax.nn.sigmoid(g) * u).astype(BF16)
        sub = min(SUB_ROWS, r)

        def down(rb):
            y_sc[rb % 2, 0:sub, :] = jnp.dot(a_sc[rb * sub:(rb + 1) * sub, :], wd_sc[...],
                                             preferred_element_type=F32)

        down(0)
        for rb in range(r // sub):
            if (rb + 1) * sub < r:
                down(rb + 1)
            rows = slice(rb * sub, (rb + 1) * sub)
            y = alpha * src_ref[rows, :].astype(F32) + y_sc[rb % 2, 0:sub, :]
            dst_ref[rows, :] = _layer_norm(y, gain_ref[...], bias_ref[...]).astype(dst_ref.dtype)

    if n_side:
        @pl.when(i == n_chunks)
        def _side_rows():
            ffn_rows(side_ref, oside_ref, n_side)

    @pl.when(i >= n_chunks)
    def _row_tile():
        ffn_rows(x_ref, o_ref, x_ref.shape[0])


def _ffn_ln(x, side, wg, wu, wd, layer, gain, bias, *, alpha, tm, out_dtype):
    rows, d = x.shape
    dff = wg.shape[2]
    tf = FFN_CHUNK
    n_chunks = dff // tf
    n_side = 0 if side is None else side.shape[0]
    chunk = lambda i: jnp.minimum(i, n_chunks - 1)
    tile = lambda i: (jnp.maximum(i - n_chunks, 0), 0)
    const = lambda i: (0, 0)
    in_specs = [pl.BlockSpec((tm, d), tile)]
    out_specs = [pl.BlockSpec((tm, d), tile)]
    out_shape = [jax.ShapeDtypeStruct((rows, d), out_dtype)]
    operands = [x]
    if n_side:
        in_specs.append(pl.BlockSpec((n_side, d), const))
        out_specs.append(pl.BlockSpec((n_side, d), const))
        out_shape.append(jax.ShapeDtypeStruct((n_side, d), out_dtype))
        operands.append(side)
    in_specs += [
        pl.BlockSpec((None, d, tf), lambda i: (layer, 0, chunk(i))),
        pl.BlockSpec((None, d, tf), lambda i: (layer, 0, chunk(i))),
        pl.BlockSpec((None, tf, d), lambda i: (layer, chunk(i), 0)),
        pl.BlockSpec((1, d), const),
        pl.BlockSpec((1, d), const),
    ]
    outs = pl.pallas_call(
        functools.partial(_ffn_ln_kernel, alpha=alpha, n_chunks=n_chunks, n_side=n_side),
        grid=(n_chunks + rows // tm,),
        in_specs=in_specs,
        out_specs=out_specs,
        out_shape=out_shape,
        scratch_shapes=[pltpu.VMEM((n_chunks, d, 2 * tf), BF16), pltpu.VMEM((dff, d), BF16),
                        pltpu.VMEM((tm, d), BF16), pltpu.VMEM((2, tm, 2 * tf), F32),
                        pltpu.VMEM((tm, dff), BF16), pltpu.VMEM((2, SUB_ROWS, d), F32)],
        compiler_params=_params("arbitrary"),
        name="ffn_ln",
    )(*operands, wg, wu, wd, gain, bias)
    return outs if n_side else (outs[0], None)


def _conv_ln_kernel(x_ref, win_ref, cw_ref, wout_ref, gain_ref, bias_ref, carry_ref,
                    o_ref, tail_ref, ubuf, proj_sc, z_sc, mix_sc, *, alpha, tiles_per_seq):
    i = pl.program_id(0)
    tm, d = x_ref.shape
    c = CONV_CARRY_ROWS
    sub = min(SUB_ROWS, tm)
    n_sub = tm // sub

    @pl.when(i % tiles_per_seq == 0)
    def _():
        ubuf[0:c, :] = carry_ref[...]

    cw = cw_ref[...]

    def in_proj(rb):
        xb = x_ref[rb * sub:(rb + 1) * sub, :].astype(BF16)
        proj_sc[rb % 2] = jnp.dot(xb, win_ref[...], preferred_element_type=F32)

    def gated_conv(rb):
        slot, base = rb % 2, c + rb * sub
        u = proj_sc[slot, :, d:2 * d] * proj_sc[slot, :, 2 * d:3 * d]
        ubuf[base:base + sub, :] = u
        y = (cw[2:3, :] * u
             + cw[1:2, :] * ubuf[base - 1:base - 1 + sub, :]
             + cw[0:1, :] * ubuf[base - 2:base - 2 + sub, :])
        z_sc[slot] = (proj_sc[slot, :, 0:d] * y).astype(BF16)

    def out_proj(rb):
        mix_sc[rb % 2] = jnp.dot(z_sc[rb % 2], wout_ref[...], preferred_element_type=F32)

    def finish(rb):
        rows = slice(rb * sub, (rb + 1) * sub)
        y = alpha * x_ref[rows, :].astype(F32) + mix_sc[rb % 2]
        o_ref[rows, :] = _layer_norm(y, gain_ref[...], bias_ref[...]).astype(o_ref.dtype)

    in_proj(0)
    for rb in range(n_sub):
        if rb + 1 < n_sub:
            in_proj(rb + 1)
        gated_conv(rb)
        out_proj(rb)
        if rb >= 1:
            finish(rb - 1)
    finish(n_sub - 1)

    tail = ubuf[tm:tm + c, :]
    ubuf[0:c, :] = tail
    tail_ref[...] = tail


def _conv_ln(x, w_in, conv_w, w_out, gain, bias, carry, *, alpha, tm, rows_per_seq):
    rows, d = x.shape
    tm = min(tm, rows_per_seq)
    const = lambda i: (0, 0)
    return pl.pallas_call(
        functools.partial(_conv_ln_kernel, alpha=alpha, tiles_per_seq=rows_per_seq // tm),
        grid=(rows // tm,),
        in_specs=[
            pl.BlockSpec((tm, d), lambda i: (i, 0)),
            pl.BlockSpec(w_in.shape, const),
            pl.BlockSpec(conv_w.shape, const),
            pl.BlockSpec(w_out.shape, const),
            pl.BlockSpec((1, d), const),
            pl.BlockSpec((1, d), const),
            pl.BlockSpec((CONV_CARRY_ROWS, d), const),
        ],
        out_specs=[pl.BlockSpec((tm, d), lambda i: (i, 0)),
                   pl.BlockSpec((CONV_CARRY_ROWS, d), const)],
        out_shape=[jax.ShapeDtypeStruct((rows, d), ACT_DTYPE),
                   jax.ShapeDtypeStruct((CONV_CARRY_ROWS, d), F32)],
        scratch_shapes=[pltpu.VMEM((tm + CONV_CARRY_ROWS, d), F32),
                        pltpu.VMEM((2, min(SUB_ROWS, tm), 3 * d), F32),
                        pltpu.VMEM((2, min(SUB_ROWS, tm), d), BF16),
                        pltpu.VMEM((2, min(SUB_ROWS, tm), d), F32)],
        compiler_params=_params("arbitrary"),
        name="conv_ln",
    )(x, w_in, conv_w, w_out, gain, bias, carry)


def _kv_kernel(x_ref, wkv_ref, wf_ref, fb_ref, cin_ref, k_ref, v_ref, c_ref, auxq_ref, auxk_ref,
               carry_sc, kv_sc, *, tiles_per_seq, n_heads, transpose_v):
    i = pl.program_id(0)
    tm, d = x_ref.shape
    sub = min(SUB_ROWS, tm)
    n_sub = tm // sub

    @pl.when(i % tiles_per_seq == 0)
    def _():
        carry_sc[...] = cin_ref[...]

    def kv_matmul(rb):
        xb = x_ref[rb * sub:(rb + 1) * sub, :].astype(BF16)
        kv_sc[rb % 2] = jnp.dot(xb, wkv_ref[...], preferred_element_type=F32)

    def kv_store(rb):
        rows = slice(rb * sub, (rb + 1) * sub)
        k_ref[rows, :] = kv_sc[rb % 2, :, 0:d].astype(BF16)
        if transpose_v:
            vt = kv_sc[rb % 2, :, d:2 * d].T.astype(BF16)
            for hh in range(n_heads):
                v_ref[0, hh * V_ROWS:hh * V_ROWS + HEAD_DIM, rows] = vt[hh * HEAD_DIM:(hh + 1) * HEAD_DIM, :]
        else:
            v_ref[rows, :] = kv_sc[rb % 2, :, d:2 * d].astype(BF16)

    f = jnp.dot(x_ref[...].astype(BF16), wf_ref[...], preferred_element_type=F32) + fb_ref[...]
    kv_matmul(0)
    if transpose_v:
        for hh in range(n_heads):
            v_ref[0, hh * V_ROWS + HEAD_DIM:(hh + 1) * V_ROWS, :] = _ones_row_tile(tm)
    log_f = jnp.minimum(f, 0.0) - jnp.log1p(jnp.exp(-jnp.abs(f)))
    row = lax.broadcasted_iota(jnp.int32, log_f.shape, 0)
    csum = log_f
    shift = 1
    while shift < tm:
        csum = csum + jnp.where(row >= shift, pltpu.roll(csum, shift, axis=0), 0.0)
        shift *= 2
    csum = csum + carry_sc[...]
    carry_sc[...] = csum[tm - 1:tm, :]
    c_ref[...] = csum

    lane = lax.broadcasted_iota(jnp.int32, csum.shape, 1)
    c2 = jnp.where(lane < n_heads, csum * LOG2E, 0.0)
    hi = c2.astype(BF16).astype(F32)
    r1 = c2 - hi
    mid = r1.astype(BF16).astype(F32)
    lo = r1 - mid
    aux = hi + pltpu.roll(mid, HEAD_GROUP, axis=1) + pltpu.roll(lo, 2 * HEAD_GROUP, axis=1)
    auxq_ref[...] = (aux.T if transpose_v else aux).astype(BF16)
    auxk_ref[...] = (-pltpu.roll(aux, AUX_K_LANE - AUX_Q_LANE, axis=1)).astype(BF16)

    for rb in range(n_sub):
        if rb + 1 < n_sub:
            kv_matmul(rb + 1)
        kv_store(rb)


def _shared_kv(x, w_kv, w_f, f_bias, c_in, *, tm, rows_per_seq, n_heads, transpose_v):
    rows, d = x.shape
    tm = min(tm, rows_per_seq)
    const = lambda i: (0, 0)
    row_tile = lambda i: (i, 0)
    if transpose_v:
        v_spec = pl.BlockSpec((1, n_heads * V_ROWS, tm), lambda i: (i, 0, 0))
        v_shape = jax.ShapeDtypeStruct((rows // tm, n_heads * V_ROWS, tm), BF16)
        auxq_spec = pl.BlockSpec((LANES, tm), lambda i: (0, i))
        auxq_shape = jax.ShapeDtypeStruct((LANES, rows), BF16)
    else:
        v_spec = pl.BlockSpec((tm, d), row_tile)
        v_shape = jax.ShapeDtypeStruct((rows, d), BF16)
        auxq_spec = pl.BlockSpec((tm, LANES), row_tile)
        auxq_shape = jax.ShapeDtypeStruct((rows, LANES), BF16)
    return pl.pallas_call(
        functools.partial(_kv_kernel, tiles_per_seq=rows_per_seq // tm, n_heads=n_heads,
                          transpose_v=transpose_v),
        grid=(rows // tm,),
        in_specs=[
            pl.BlockSpec((tm, d), row_tile),
            pl.BlockSpec(w_kv.shape, const),
            pl.BlockSpec(w_f.shape, const),
            pl.BlockSpec((1, LANES), const),
            pl.BlockSpec((1, LANES), const),
        ],
        out_specs=[pl.BlockSpec((tm, d), row_tile), v_spec,
                   pl.BlockSpec((tm, LANES), row_tile),
                   auxq_spec,
                   pl.BlockSpec((tm, LANES), row_tile)],
        out_shape=[jax.ShapeDtypeStruct((rows, d), BF16), v_shape,
                   jax.ShapeDtypeStruct((rows, LANES), F32),
                   auxq_shape,
                   jax.ShapeDtypeStruct((rows, LANES), BF16)],
        scratch_shapes=[pltpu.VMEM((1, LANES), F32),
                        pltpu.VMEM((2, min(SUB_ROWS, tm), 2 * d), F32)],
        compiler_params=_params("arbitrary"),
        name="shared_kv",
    )(x, w_kv, w_f, f_bias, c_in)


def _sub_block_pipeline(n_sub, matmul, epilogue):
    matmul(0)
    for rb in range(n_sub):
        if rb + 1 < n_sub:
            matmul(rb + 1)
        epilogue(rb)


def _qproj_kernel(x_ref, w_ref, qt_ref, y_sc, *, scale):
    tm = x_ref.shape[0]
    sub = min(SUB_ROWS, tm)

    def matmul(rb):
        xb = x_ref[rb * sub:(rb + 1) * sub, :].astype(BF16)
        y_sc[rb % 2] = jnp.dot(xb, w_ref[...], preferred_element_type=F32)

    def epilogue(rb):
        qt_ref[:, rb * sub:(rb + 1) * sub] = (y_sc[rb % 2] * scale).T.astype(qt_ref.dtype)

    _sub_block_pipeline(tm // sub, matmul, epilogue)


def _qproj(x, w, *, scale, tm):
    rows, d = x.shape
    dq = w.shape[1]
    return pl.pallas_call(
        functools.partial(_qproj_kernel, scale=scale),
        grid=(rows // tm,),
        in_specs=[pl.BlockSpec((tm, d), lambda i: (i, 0)),
                  pl.BlockSpec(w.shape, lambda i: (0, 0))],
        out_specs=pl.BlockSpec((dq, tm), lambda i: (0, i)),
        out_shape=jax.ShapeDtypeStruct((dq, rows), BF16),
        scratch_shapes=[pltpu.VMEM((2, min(SUB_ROWS, tm), dq), F32)],
        compiler_params=_params("parallel"),
        name="qproj",
    )(x, w)


def _head_one_hot(h, base, axis):
    shape = (1, LANES) if axis == 1 else (LANES, 1)
    pos = lax.broadcasted_iota(jnp.int32, shape, axis)
    hit = (pos == base + h) | (pos == base + HEAD_GROUP + h) | (pos == base + 2 * HEAD_GROUP + h)
    return jnp.where(hit, 1.0, 0.0).astype(BF16)


def _attn_kernel(qt_ref, auxqt_ref, k_ref, auxk_ref, vt_ref, km_ref, auxkm_ref, vtm_ref, o_ref,
                 qa_sc, s_sc, m_sc, acc_sc, *, tk):
    h = pl.program_id(1)
    qi = pl.program_id(2)
    tq = qt_ref.shape[1]
    ts = tk
    n_strips = tq // ts

    q_sel = _head_one_hot(h, AUX_K_LANE, 0)
    k_sel = _head_one_hot(h, AUX_Q_LANE, 1)
    qa_sc[0:HEAD_DIM, :] = qt_ref[...]
    qa_sc[HEAD_DIM:, :] = auxqt_ref[...] + q_sel
    m_sc[...] = jnp.full(m_sc.shape, -MASK_VALUE, F32)
    acc_sc[...] = jnp.zeros(acc_sc.shape, F32)

    def strip_scores(t, ka, c, causal):
        cs = slice(c * ts, (c + 1) * ts)
        s = jnp.dot(ka, qa_sc[:, cs], preferred_element_type=F32)
        if causal:
            key = lax.broadcasted_iota(jnp.int32, s.shape, 0)
            qry = lax.broadcasted_iota(jnp.int32, s.shape, 1)
            s = jnp.where(key <= qry, s, -MASK_VALUE)
        s_sc[t % SCORE_SLOTS, 0:s.shape[0], :] = s
        return jnp.max(s, axis=0, keepdims=True)

    def strip_update(t, s_max, vt, c):
        cs = slice(c * ts, (c + 1) * ts)
        m = m_sc[:, cs]
        m_new = jnp.maximum(m, s_max)
        corr = jnp.exp2(m - m_new)
        m_sc[:, cs] = m_new
        p = jnp.exp2(s_sc[t % SCORE_SLOTS, 0:vt.shape[1], :] - m_new)
        pv = jnp.dot(vt, p.astype(BF16), preferred_element_type=F32)
        acc_sc[:, cs] = corr * acc_sc[:, cs] + pv

    def run_tasks(tasks):
        pending = [strip_scores(t, ka, c, causal)
                   for t, (ka, _, c, causal) in enumerate(tasks[:QK_LOOKAHEAD])]
        for t, (_, vt, c, _) in enumerate(tasks):
            if t + QK_LOOKAHEAD < len(tasks):
                ka_n, _, c_n, causal_n = tasks[t + QK_LOOKAHEAD]
                pending.append(strip_scores(t + QK_LOOKAHEAD, ka_n, c_n, causal_n))
            strip_update(t, pending.pop(0), vt, c)

    def key_block(j):
        off = pl.multiple_of(j * tk, tk)
        ka = jnp.concatenate([k_ref[pl.ds(off, tk), :], auxk_ref[pl.ds(off, tk), :] + k_sel], axis=1)
        return ka, vt_ref[j]

    def full_blocks(jj, _):
        blocks = [key_block(jj * KV_UNROLL + u) for u in range(KV_UNROLL)]
        run_tasks([(ka, vt, c, False) for ka, vt in blocks for c in range(n_strips)])
        return 0

    assert n_strips % KV_UNROLL == 0
    lax.fori_loop(0, qi * (n_strips // KV_UNROLL), full_blocks, 0)

    diag = [key_block(qi * n_strips + d) for d in range(n_strips)]
    ka_m = jnp.concatenate([km_ref[...], auxkm_ref[...] + k_sel], axis=1)
    run_tasks([(diag[d][0], diag[d][1], c, c == d) for d in range(n_strips) for c in range(d, n_strips)]
              + [(ka_m, vtm_ref[...], c, False) for c in range(n_strips)])

    denom = acc_sc[HEAD_DIM:HEAD_DIM + 1, :]
    o_ref[...] = (acc_sc[0:HEAD_DIM, :] / denom).T.astype(o_ref.dtype)


def _attention(qt, auxqt, k, auxk, vt, km, auxkm, vtm, *, batch, seq, n_heads, tq, tk):
    d, rows = qt.shape
    nq = seq // tq
    nk = seq // tk
    assert vt.shape == (batch * nk, n_heads * V_ROWS, tk)
    return pl.pallas_call(
        functools.partial(_attn_kernel, tk=tk),
        grid=(batch, n_heads, nq),
        in_specs=[
            pl.BlockSpec((HEAD_DIM, tq), lambda b, h, i: (h, b * nq + i)),
            pl.BlockSpec((LANES, tq), lambda b, h, i: (0, b * nq + i)),
            pl.BlockSpec((seq, HEAD_DIM), lambda b, h, i: (b, h)),
            pl.BlockSpec((seq, LANES), lambda b, h, i: (b, 0)),
            pl.BlockSpec((nk, V_ROWS, tk), lambda b, h, i: (b, h, 0)),
            pl.BlockSpec((LANES, HEAD_DIM), lambda b, h, i: (0, h)),
            pl.BlockSpec((LANES, LANES), lambda b, h, i: (0, 0)),
            pl.BlockSpec((V_ROWS, LANES), lambda b, h, i: (h, 0)),
        ],
        out_specs=pl.BlockSpec((tq, HEAD_DIM), lambda b, h, i: (b * nq + i, h)),
        out_shape=jax.ShapeDtypeStruct((rows, d), BF16),
        scratch_shapes=[pltpu.VMEM((HEAD_DIM + LANES, tq), BF16),
                        pltpu.VMEM((SCORE_SLOTS, tk, tk), F32),
                        pltpu.VMEM((1, tq), F32),
                        pltpu.VMEM((V_ROWS, tq), F32)],
        compiler_params=_params("parallel", "parallel", "arbitrary"),
        name="fox_attention",
    )(qt, auxqt, k, auxk, vt, km, auxkm, vtm)


def _oproj_ln_kernel(o_ref, x_ref, w_ref, gain_ref, bias_ref, y_ref, mix_sc, *, alpha):
    tm = x_ref.shape[0]
    sub = min(SUB_ROWS, tm)

    def matmul(rb):
        mix_sc[rb % 2] = jnp.dot(o_ref[rb * sub:(rb + 1) * sub, :], w_ref[...],
                                 preferred_element_type=F32)

    def epilogue(rb):
        rows = slice(rb * sub, (rb + 1) * sub)
        y = alpha * x_ref[rows, :].astype(F32) + mix_sc[rb % 2]
        y_ref[rows, :] = _layer_norm(y, gain_ref[...], bias_ref[...]).astype(y_ref.dtype)

    _sub_block_pipeline(tm // sub, matmul, epilogue)


def _oproj_ln(o, x, w, gain, bias, *, alpha, tm):
    rows, d = x.shape
    const = lambda i: (0, 0)
    return pl.pallas_call(
        functools.partial(_oproj_ln_kernel, alpha=alpha),
        grid=(rows // tm,),
        in_specs=[pl.BlockSpec((tm, d), lambda i: (i, 0)),
                  pl.BlockSpec((tm, d), lambda i: (i, 0)),
                  pl.BlockSpec(w.shape, const),
                  pl.BlockSpec((1, d), const),
                  pl.BlockSpec((1, d), const)],
        out_specs=pl.BlockSpec((tm, d), lambda i: (i, 0)),
        out_shape=jax.ShapeDtypeStruct((rows, d), ACT_DTYPE),
        scratch_shapes=[pltpu.VMEM((2, min(SUB_ROWS, tm), d), F32)],
        compiler_params=_params("parallel"),
        name="oproj_ln",
    )(o, x, w, gain, bias)


def kernel(x, meta, ffn1_wg, ffn1_wu, ffn1_wd, ffn2_wg, ffn2_wu, ffn2_wd, ln_gain, ln_bias,
           conv_w_in, conv_w, conv_w_out, kv_w, f_bias, attn_w_q, attn_w_o):
    batch, seq, d = x.shape
    n_meta = meta.shape[0]
    depth = ffn1_wg.shape[0]
    n_heads = f_bias.shape[0]
    assert depth == 2 and conv_w_in.shape[0] == 1 and attn_w_q.shape[0] == 1
    assert d == n_heads * HEAD_DIM and n_heads == HEAD_GROUP
    assert n_meta <= LANES and n_meta % 16 == 0
    alpha = (2 * depth) ** 0.25
    q_scale = LOG2E / math.sqrt(HEAD_DIM)
    attn_tile = 512

    bf = lambda w: w.astype(BF16)
    gain = lambda l, i: ln_gain[l, i].reshape(1, d).astype(F32)
    bias = lambda l, i: ln_bias[l, i].reshape(1, d).astype(F32)
    ffn_w = {1: (ffn1_wg, ffn1_wu, ffn1_wd), 2: (ffn2_wg, ffn2_wu, ffn2_wd)}

    def ffn(h, side, which, l, out_dtype=ACT_DTYPE):
        ln_idx = 0 if which == 1 else 2
        return _ffn_ln(h, side, *ffn_w[which], l, gain(l, ln_idx), bias(l, ln_idx),
                       alpha=alpha, tm=1024, out_dtype=out_dtype)

    xs = x.reshape(batch * seq, d)
    ms = meta.astype(x.dtype)

    w_in, w_out, cw = bf(conv_w_in[0]), bf(conv_w_out[0]), conv_w[0].astype(F32)
    conv = functools.partial(_conv_ln, alpha=alpha, tm=1024)
    h1, m1 = ffn(xs, ms, 1, 0)
    zero_carry = jnp.zeros((CONV_CARRY_ROWS, d), F32)
    m2, m_tail = conv(m1, w_in, cw, w_out, gain(0, 1), bias(0, 1), zero_carry, rows_per_seq=n_meta)
    h2, _ = conv(h1, w_in, cw, w_out, gain(0, 1), bias(0, 1), m_tail, rows_per_seq=seq)
    h3, m3 = ffn(h2, m2, 2, 0)

    w_kv = bf(kv_w[:, :2 * d])
    w_f = jnp.pad(bf(kv_w[:, 2 * d:]), ((0, 0), (0, LANES - n_heads)))
    fb = jnp.pad(f_bias.astype(F32), (0, LANES - n_heads)).reshape(1, LANES)
    kv = functools.partial(_shared_kv, tm=attn_tile, n_heads=n_heads)
    km, vm, cm, _, auxkm = kv(m3, w_kv, w_f, fb, jnp.zeros((1, LANES), F32),
                              rows_per_seq=n_meta, transpose_v=False)
    k, vt, _, auxq, auxk = kv(h3, w_kv, w_f, fb, cm[n_meta - 1:n_meta],
                              rows_per_seq=seq, transpose_v=True)
    pad_aux = np.zeros((LANES - n_meta, LANES), np.float32)
    pad_aux[:, AUX_K_LANE:AUX_K_LANE + n_heads] = -MASK_VALUE
    km = jnp.pad(km, ((0, LANES - n_meta), (0, 0)))
    auxkm = jnp.concatenate([auxkm, jnp.asarray(pad_aux, BF16)], axis=0)
    ones_tile = np.zeros((n_heads, BF16_SUBLANES, LANES), np.float32)
    ones_tile[:, 0, :] = 1.0
    vtm = jnp.pad(vm.T, ((0, 0), (0, LANES - n_meta))).reshape(n_heads, HEAD_DIM, LANES)
    vtm = jnp.concatenate([vtm, jnp.asarray(ones_tile, BF16)], axis=1).reshape(n_heads * V_ROWS, LANES)

    h4, _ = ffn(h3, None, 1, 1)
    q = _qproj(h4, bf(attn_w_q[0]), scale=q_scale, tm=1024)
    o = _attention(q, auxq, k, auxk, vt, km, auxkm, vtm, batch=batch, seq=seq, n_heads=n_heads,
                   tq=8 * attn_tile, tk=attn_tile)
    h5 = _oproj_ln(o, h4, bf(attn_w_o[0]), gain(1, 1), bias(1, 1), alpha=alpha, tm=1024)
    out, _ = ffn(h5, None, 2, 1, out_dtype=x.dtype)
    return out.reshape(batch, seq, d)
```

```python
import functools
import math

import jax
import jax.numpy as jnp
import numpy as np
from jax import lax
from jax.experimental import pallas as pl
from jax.experimental.pallas import tpu as pltpu

F32 = jnp.float32
BF16 = jnp.bfloat16

LN_EPS = 1e-5
MASK_VALUE = 1e30
LOG2E = math.log2(math.e)
HEAD_DIM = 128
LANES = 128
CONV_CARRY_ROWS = 8
FFN_CHUNK = 256
SUB_ROWS = 256
ACT_DTYPE = BF16
VMEM_LIMIT_BYTES = 56 * 1024 * 1024

N_SPLIT = 3
HEAD_GROUP = 8
AUX_Q_LANE = 0
AUX_K_LANE = N_SPLIT * HEAD_GROUP
QK_LOOKAHEAD = 2
SCORE_SLOTS = QK_LOOKAHEAD + 2
KV_UNROLL = 4
BF16_SUBLANES = 16
V_ROWS = HEAD_DIM + BF16_SUBLANES


def _ones_row_tile(width):
    row = lax.broadcasted_iota(jnp.int32, (BF16_SUBLANES, width), 0)
    return jnp.where(row == 0, 1.0, 0.0).astype(BF16)


def _params(*semantics):
    return pltpu.CompilerParams(dimension_semantics=semantics, vmem_limit_bytes=VMEM_LIMIT_BYTES)


def _layer_norm(y, gain, bias):
    mu = jnp.mean(y, axis=-1, keepdims=True)
    yc = y - mu
    var = jnp.mean(yc * yc, axis=-1, keepdims=True)
    return yc * lax.rsqrt(var + LN_EPS) * gain + bias


def _ffn_ln_kernel(*refs, alpha, n_chunks, n_side):
    if n_side:
        (x_ref, side_ref, wg_ref, wu_ref, wd_ref, gain_ref, bias_ref, o_ref, oside_ref,
         wgu_sc, wd_sc, xb_sc, gu_sc, a_sc, y_sc, aside_sc) = refs
    else:
        (x_ref, wg_ref, wu_ref, wd_ref, gain_ref, bias_ref, o_ref,
         wgu_sc, wd_sc, xb_sc, gu_sc, a_sc, y_sc) = refs
    i = pl.program_id(0)
    tf = wg_ref.shape[1]

    @pl.when(i < n_chunks)
    def _stage_weights():
        wgu_sc[i, :, 0:tf] = wg_ref[...].astype(BF16)
        wgu_sc[i, :, tf:2 * tf] = wu_ref[...].astype(BF16)
        wd_sc[pl.ds(pl.multiple_of(i * tf, tf), tf), :] = (0.5 * wd_ref[...]).astype(BF16)
        if n_side:
            gu = jnp.dot(side_ref[...].astype(BF16), wgu_sc[i], preferred_element_type=F32)
            g, u = gu[:, 0:tf], gu[:, tf:2 * tf]
            aside_sc[i] = (g * jax.nn.sigmoid(g) * u).astype(BF16)

    if n_side:
        @pl.when(i == n_chunks - 1)
        def _side_rows_finish():
            half_ffn = jnp.dot(aside_sc[0], wd_sc[0:tf, :], preferred_element_type=F32)
            for c in range(1, n_chunks):
                half_ffn += jnp.dot(aside_sc[c], wd_sc[c * tf:(c + 1) * tf, :],
                                    preferred_element_type=F32)
            y = alpha * side_ref[...].astype(F32) + half_ffn
            oside_ref[...] = _layer_norm(y, gain_ref[...], bias_ref[...]).astype(oside_ref.dtype)

    def ffn_rows(src_ref, dst_ref, r):
        if src_ref.dtype == BF16:
            lhs_ref = src_ref
        else:
            xb_sc[0:r, :] = src_ref[...].astype(BF16)
            lhs_ref = xb_sc.at[0:r, :]

        def gate_up(c):
            gu_sc[c % 2, 0:r, :] = jnp.dot(lhs_ref[...], wgu_sc[c], preferred_element_type=F32)

        gate_up(0)
        for c in range(n_chunks):
            if c + 1 < n_chunks:
                gate_up(c + 1)
            g = gu_sc[c % 2, 0:r, 0:tf]
            u = gu_sc[c % 2, 0:r, tf:2 * tf]
            a_sc[0:r, c * tf:(c + 1) * tf] = (g * jax.nn.sigmoid(g) * u).astype(BF16)
        sub = min(SUB_ROWS, r)

        def down(rb):
            y_sc[rb % 2, 0:sub, :] = jnp.dot(a_sc[rb * sub:(rb + 1) * sub, :], wd_sc[...],
                                             preferred_element_type=F32)

        down(0)
        for rb in range(r // sub):
            if (rb + 1) * sub < r:
                down(rb + 1)
            rows = slice(rb * sub, (rb + 1) * sub)
            y = alpha * src_ref[rows, :].astype(F32) + y_sc[rb % 2, 0:sub, :]
            dst_ref[rows, :] = _layer_norm(y, gain_ref[...], bias_ref[...]).astype(dst_ref.dtype)

    @pl.when(i >= n_chunks)
    def _row_tile():
        ffn_rows(x_ref, o_ref, x_ref.shape[0])


def _ffn_ln(x, side, wg, wu, wd, layer, gain, bias, *, alpha, tm, out_dtype):
    rows, d = x.shape
    dff = wg.shape[2]
    tf = FFN_CHUNK
    n_chunks = dff // tf
    n_side = 0 if side is None else side.shape[0]
    chunk = lambda i: jnp.minimum(i, n_chunks - 1)
    tile = lambda i: (jnp.maximum(i - n_chunks, 0), 0)
    const = lambda i: (0, 0)
    in_specs = [pl.BlockSpec((tm, d), tile)]
    out_specs = [pl.BlockSpec((tm, d), tile)]
    out_shape = [jax.ShapeDtypeStruct((rows, d), out_dtype)]
    operands = [x]
    if n_side:
        in_specs.append(pl.BlockSpec((n_side, d), const))
        out_specs.append(pl.BlockSpec((n_side, d), const))
        out_shape.append(jax.ShapeDtypeStruct((n_side, d), out_dtype))
        operands.append(side)
    in_specs += [
        pl.BlockSpec((None, d, tf), lambda i: (layer, 0, chunk(i))),
        pl.BlockSpec((None, d, tf), lambda i: (layer, 0, chunk(i))),
        pl.BlockSpec((None, tf, d), lambda i: (layer, chunk(i), 0)),
        pl.BlockSpec((1, d), const),
        pl.BlockSpec((1, d), const),
    ]
    scratch_shapes = [pltpu.VMEM((n_chunks, d, 2 * tf), BF16), pltpu.VMEM((dff, d), BF16),
                      pltpu.VMEM((tm, d), BF16), pltpu.VMEM((2, tm, 2 * tf), F32),
                      pltpu.VMEM((tm, dff), BF16), pltpu.VMEM((2, SUB_ROWS, d), F32)]
    if n_side:
        scratch_shapes.append(pltpu.VMEM((n_chunks, n_side, tf), BF16))
    outs = pl.pallas_call(
        functools.partial(_ffn_ln_kernel, alpha=alpha, n_chunks=n_chunks, n_side=n_side),
        grid=(n_chunks + rows // tm,),
        in_specs=in_specs,
        out_specs=out_specs,
        out_shape=out_shape,
        scratch_shapes=scratch_shapes,
        compiler_params=_params("arbitrary"),
        name="ffn_ln",
    )(*operands, wg, wu, wd, gain, bias)
    return outs if n_side else (outs[0], None)


def _conv_ln_kernel(x_ref, win_ref, cw_ref, wout_ref, gain_ref, bias_ref, carry_ref,
                    o_ref, tail_ref, ubuf, proj_sc, z_sc, mix_sc, *, alpha, tiles_per_seq):
    i = pl.program_id(0)
    tm, d = x_ref.shape
    c = CONV_CARRY_ROWS
    sub = min(SUB_ROWS, tm)
    n_sub = tm // sub

    @pl.when(i % tiles_per_seq == 0)
    def _():
        ubuf[0:c, :] = carry_ref[...]

    cw = cw_ref[...]

    def in_proj(rb):
        xb = x_ref[rb * sub:(rb + 1) * sub, :].astype(BF16)
        proj_sc[rb % 2] = jnp.dot(xb, win_ref[...], preferred_element_type=F32)

    def gated_conv(rb):
        slot, base = rb % 2, c + rb * sub
        u = proj_sc[slot, :, d:2 * d] * proj_sc[slot, :, 2 * d:3 * d]
        ubuf[base:base + sub, :] = u
        y = (cw[2:3, :] * u
             + cw[1:2, :] * ubuf[base - 1:base - 1 + sub, :]
             + cw[0:1, :] * ubuf[base - 2:base - 2 + sub, :])
        z_sc[slot] = (proj_sc[slot, :, 0:d] * y).astype(BF16)

    def out_proj(rb):
        mix_sc[rb % 2] = jnp.dot(z_sc[rb % 2], wout_ref[...], preferred_element_type=F32)

    def finish(rb):
        rows = slice(rb * sub, (rb + 1) * sub)
        y = alpha * x_ref[rows, :].astype(F32) + mix_sc[rb % 2]
        o_ref[rows, :] = _layer_norm(y, gain_ref[...], bias_ref[...]).astype(o_ref.dtype)

    in_proj(0)
    for rb in range(n_sub):
        if rb + 1 < n_sub:
            in_proj(rb + 1)
        gated_conv(rb)
        out_proj(rb)
        if rb >= 1:
            finish(rb - 1)
    finish(n_sub - 1)

    tail = ubuf[tm:tm + c, :]
    ubuf[0:c, :] = tail
    tail_ref[...] = tail


def _conv_ln(x, w_in, conv_w, w_out, gain, bias, carry, *, alpha, tm, rows_per_seq):
    rows, d = x.shape
    tm = min(tm, rows_per_seq)
    const = lambda i: (0, 0)
    return pl.pallas_call(
        functools.partial(_conv_ln_kernel, alpha=alpha, tiles_per_seq=rows_per_seq // tm),
        grid=(rows // tm,),
        in_specs=[
            pl.BlockSpec((tm, d), lambda i: (i, 0)),
            pl.BlockSpec(w_in.shape, const),
            pl.BlockSpec(conv_w.shape, const),
            pl.BlockSpec(w_out.shape, const),
            pl.BlockSpec((1, d), const),
            pl.BlockSpec((1, d), const),
            pl.BlockSpec((CONV_CARRY_ROWS, d), const),
        ],
        out_specs=[pl.BlockSpec((tm, d), lambda i: (i, 0)),
                   pl.BlockSpec((CONV_CARRY_ROWS, d), const)],
        out_shape=[jax.ShapeDtypeStruct((rows, d), ACT_DTYPE),
                   jax.ShapeDtypeStruct((CONV_CARRY_ROWS, d), F32)],
        scratch_shapes=[pltpu.VMEM((tm + CONV_CARRY_ROWS, d), F32),
                        pltpu.VMEM((2, min(SUB_ROWS, tm), 3 * d), F32),
                        pltpu.VMEM((2, min(SUB_ROWS, tm), d), BF16),
                        pltpu.VMEM((2, min(SUB_ROWS, tm), d), F32)],
        compiler_params=_params("arbitrary"),
        name="conv_ln",
    )(x, w_in, conv_w, w_out, gain, bias, carry)


def _kv_kernel(x_ref, wkv_ref, wf_ref, fb_ref, cin_ref, k_ref, v_ref, c_ref, auxq_ref, auxk_ref,
               carry_sc, kv_sc, *, tiles_per_seq, n_heads, transpose_v):
    i = pl.program_id(0)
    tm, d = x_ref.shape
    sub = min(SUB_ROWS, tm)
    n_sub = tm // sub

    @pl.when(i % tiles_per_seq == 0)
    def _():
        carry_sc[...] = cin_ref[...]

    def kv_matmul(rb):
        xb = x_ref[rb * sub:(rb + 1) * sub, :].astype(BF16)
        kv_sc[rb % 2] = jnp.dot(xb, wkv_ref[...], preferred_element_type=F32)

    def kv_store(rb):
        rows = slice(rb * sub, (rb + 1) * sub)
        k_ref[rows, :] = kv_sc[rb % 2, :, 0:d].astype(BF16)
        if transpose_v:
            vt = kv_sc[rb % 2, :, d:2 * d].T.astype(BF16)
            vt_block = v_ref.shape[2]
            blk, col = divmod(rb * sub, vt_block)
            for hh in range(n_heads):
                v_ref[blk, hh * V_ROWS:hh * V_ROWS + HEAD_DIM, col:col + sub] = (
                    vt[hh * HEAD_DIM:(hh + 1) * HEAD_DIM, :])
        else:
            v_ref[rows, :] = kv_sc[rb % 2, :, d:2 * d].astype(BF16)

    f = jnp.dot(x_ref[...].astype(BF16), wf_ref[...], preferred_element_type=F32) + fb_ref[...]
    kv_matmul(0)
    if transpose_v:
        for blk in range(v_ref.shape[0]):
            for hh in range(n_heads):
                v_ref[blk, hh * V_ROWS + HEAD_DIM:(hh + 1) * V_ROWS, :] = _ones_row_tile(v_ref.shape[2])
    log_f = jnp.minimum(f, 0.0) - jnp.log1p(jnp.exp(-jnp.abs(f)))
    row = lax.broadcasted_iota(jnp.int32, log_f.shape, 0)
    csum = log_f
    shift = 1
    while shift < tm:
        csum = csum + jnp.where(row >= shift, pltpu.roll(csum, shift, axis=0), 0.0)
        shift *= 2
    csum = csum + carry_sc[...]
    carry_sc[...] = csum[tm - 1:tm, :]
    c_ref[...] = csum

    lane = lax.broadcasted_iota(jnp.int32, csum.shape, 1)
    c2 = jnp.where(lane < n_heads, csum * LOG2E, 0.0)
    hi = c2.astype(BF16).astype(F32)
    r1 = c2 - hi
    mid = r1.astype(BF16).astype(F32)
    lo = r1 - mid
    aux = hi + pltpu.roll(mid, HEAD_GROUP, axis=1) + pltpu.roll(lo, 2 * HEAD_GROUP, axis=1)
    auxq_ref[...] = (aux.T if transpose_v else aux).astype(BF16)
    auxk_ref[...] = (-pltpu.roll(aux, AUX_K_LANE - AUX_Q_LANE, axis=1)).astype(BF16)

    for rb in range(n_sub):
        if rb + 1 < n_sub:
            kv_matmul(rb + 1)
        kv_store(rb)


def _shared_kv(x, w_kv, w_f, f_bias, c_in, *, tm, rows_per_seq, n_heads, vt_block):
    rows, d = x.shape
    tm = min(tm, rows_per_seq)
    transpose_v = vt_block is not None
    const = lambda i: (0, 0)
    row_tile = lambda i: (i, 0)
    if transpose_v:
        v_spec = pl.BlockSpec((tm // vt_block, n_heads * V_ROWS, vt_block), lambda i: (i, 0, 0))
        v_shape = jax.ShapeDtypeStruct((rows // vt_block, n_heads * V_ROWS, vt_block), BF16)
        auxq_spec = pl.BlockSpec((LANES, tm), lambda i: (0, i))
        auxq_shape = jax.ShapeDtypeStruct((LANES, rows), BF16)
    else:
        v_spec = pl.BlockSpec((tm, d), row_tile)
        v_shape = jax.ShapeDtypeStruct((rows, d), BF16)
        auxq_spec = pl.BlockSpec((tm, LANES), row_tile)
        auxq_shape = jax.ShapeDtypeStruct((rows, LANES), BF16)
    return pl.pallas_call(
        functools.partial(_kv_kernel, tiles_per_seq=rows_per_seq // tm, n_heads=n_heads,
                          transpose_v=transpose_v),
        grid=(rows // tm,),
        in_specs=[
            pl.BlockSpec((tm, d), row_tile),
            pl.BlockSpec(w_kv.shape, const),
            pl.BlockSpec(w_f.shape, const),
            pl.BlockSpec((1, LANES), const),
            pl.BlockSpec((1, LANES), const),
        ],
        out_specs=[pl.BlockSpec((tm, d), row_tile), v_spec,
                   pl.BlockSpec((tm, LANES), row_tile),
                   auxq_spec,
                   pl.BlockSpec((tm, LANES), row_tile)],
        out_shape=[jax.ShapeDtypeStruct((rows, d), BF16), v_shape,
                   jax.ShapeDtypeStruct((rows, LANES), F32),
                   auxq_shape,
                   jax.ShapeDtypeStruct((rows, LANES), BF16)],
        scratch_shapes=[pltpu.VMEM((1, LANES), F32),
                        pltpu.VMEM((2, min(SUB_ROWS, tm), 2 * d), F32)],
        compiler_params=_params("arbitrary"),
        name="shared_kv",
    )(x, w_kv, w_f, f_bias, c_in)


def _sub_block_pipeline(n_sub, matmul, epilogue):
    matmul(0)
    for rb in range(n_sub):
        if rb + 1 < n_sub:
            matmul(rb + 1)
        epilogue(rb)


def _qproj_kernel(x_ref, w_ref, qt_ref, y_sc, *, scale):
    tm = x_ref.shape[0]
    sub = min(SUB_ROWS, tm)

    def matmul(rb):
        xb = x_ref[rb * sub:(rb + 1) * sub, :].astype(BF16)
        y_sc[rb % 2] = jnp.dot(xb, w_ref[...], preferred_element_type=F32)

    def epilogue(rb):
        qt_ref[:, rb * sub:(rb + 1) * sub] = (y_sc[rb % 2] * scale).T.astype(qt_ref.dtype)

    _sub_block_pipeline(tm // sub, matmul, epilogue)


def _qproj(x, w, *, scale, tm):
    rows, d = x.shape
    dq = w.shape[1]
    return pl.pallas_call(
        functools.partial(_qproj_kernel, scale=scale),
        grid=(rows // tm,),
        in_specs=[pl.BlockSpec((tm, d), lambda i: (i, 0)),
                  pl.BlockSpec(w.shape, lambda i: (0, 0))],
        out_specs=pl.BlockSpec((dq, tm), lambda i: (0, i)),
        out_shape=jax.ShapeDtypeStruct((dq, rows), BF16),
        scratch_shapes=[pltpu.VMEM((2, min(SUB_ROWS, tm), dq), F32)],
        compiler_params=_params("parallel"),
        name="qproj",
    )(x, w)


def _head_one_hot(h, base, axis):
    shape = (1, LANES) if axis == 1 else (LANES, 1)
    pos = lax.broadcasted_iota(jnp.int32, shape, axis)
    hit = (pos == base + h) | (pos == base + HEAD_GROUP + h) | (pos == base + 2 * HEAD_GROUP + h)
    return jnp.where(hit, 1.0, 0.0).astype(BF16)


def _attn_kernel(qt_ref, auxqt_ref, k_ref, auxk_ref, vt_ref, km_ref, auxkm_ref, vtm_ref, o_ref,
                 qa_sc, s_sc, m_sc, acc_sc, *, tk):
    h = pl.program_id(1)
    qi = pl.program_id(2)
    tq = qt_ref.shape[1]
    ts = tk
    n_strips = tq // ts

    q_sel = _head_one_hot(h, AUX_K_LANE, 0)
    k_sel = _head_one_hot(h, AUX_Q_LANE, 1)
    qa_sc[0:HEAD_DIM, :] = qt_ref[...]
    qa_sc[HEAD_DIM:, :] = auxqt_ref[...] + q_sel
    m_sc[...] = jnp.full(m_sc.shape, -MASK_VALUE, F32)
    acc_sc[...] = jnp.zeros(acc_sc.shape, F32)

    def strip_scores(t, ka, c, causal):
        cs = slice(c * ts, (c + 1) * ts)
        s = jnp.dot(ka, qa_sc[:, cs], preferred_element_type=F32)
        if causal:
            key = lax.broadcasted_iota(jnp.int32, s.shape, 0)
            qry = lax.broadcasted_iota(jnp.int32, s.shape, 1)
            s = jnp.where(key <= qry, s, -MASK_VALUE)
        s_sc[t % SCORE_SLOTS, 0:s.shape[0], :] = s
        return jnp.max(s, axis=0, keepdims=True)

    def strip_update(t, s_max, vt, c):
        cs = slice(c * ts, (c + 1) * ts)
        m = m_sc[:, cs]
        m_new = jnp.maximum(m, s_max)
        corr = jnp.exp2(m - m_new)
        m_sc[:, cs] = m_new
        p = jnp.exp2(s_sc[t % SCORE_SLOTS, 0:vt.shape[1], :] - m_new)
        pv = jnp.dot(vt, p.astype(BF16), preferred_element_type=F32)
        acc_sc[:, cs] = corr * acc_sc[:, cs] + pv

    def run_tasks(tasks):
        pending = [strip_scores(t, ka, c, causal)
                   for t, (ka, _, c, causal) in enumerate(tasks[:QK_LOOKAHEAD])]
        for t, (_, vt, c, _) in enumerate(tasks):
            if t + QK_LOOKAHEAD < len(tasks):
                ka_n, _, c_n, causal_n = tasks[t + QK_LOOKAHEAD]
                pending.append(strip_scores(t + QK_LOOKAHEAD, ka_n, c_n, causal_n))
            strip_update(t, pending.pop(0), vt, c)

    def key_block(j):
        off = pl.multiple_of(j * tk, tk)
        ka = jnp.concatenate([k_ref[pl.ds(off, tk), :], auxk_ref[pl.ds(off, tk), :] + k_sel], axis=1)
        return ka, vt_ref[j]

    def full_blocks(jj, _):
        blocks = [key_block(jj * KV_UNROLL + u) for u in range(KV_UNROLL)]
        run_tasks([(ka, vt, c, False) for ka, vt in blocks for c in range(n_strips)])
        return 0

    assert n_strips % KV_UNROLL == 0
    lax.fori_loop(0, qi * (n_strips // KV_UNROLL), full_blocks, 0)

    diag = [key_block(qi * n_strips + d) for d in range(n_strips)]
    ka_m = jnp.concatenate([km_ref[...], auxkm_ref[...] + k_sel], axis=1)
    run_tasks([(diag[d][0], diag[d][1], c, c == d) for d in range(n_strips) for c in range(d, n_strips)]
              + [(ka_m, vtm_ref[...], c, False) for c in range(n_strips)])

    denom = acc_sc[HEAD_DIM:HEAD_DIM + 1, :]
    o_ref[...] = (acc_sc[0:HEAD_DIM, :] / denom).T.astype(o_ref.dtype)


def _attention(qt, auxqt, k, auxk, vt, km, auxkm, vtm, *, batch, seq, n_heads, tq, tk):
    d, rows = qt.shape
    nq = seq // tq
    nk = seq // tk
    assert vt.shape == (batch * nk, n_heads * V_ROWS, tk)
    return pl.pallas_call(
        functools.partial(_attn_kernel, tk=tk),
        grid=(batch, n_heads, nq),
        in_specs=[
            pl.BlockSpec((HEAD_DIM, tq), lambda b, h, i: (h, b * nq + i)),
            pl.BlockSpec((LANES, tq), lambda b, h, i: (0, b * nq + i)),
            pl.BlockSpec((seq, HEAD_DIM), lambda b, h, i: (b, h)),
            pl.BlockSpec((seq, LANES), lambda b, h, i: (b, 0)),
            pl.BlockSpec((nk, V_ROWS, tk), lambda b, h, i: (b, h, 0)),
            pl.BlockSpec((LANES, HEAD_DIM), lambda b, h, i: (0, h)),
            pl.BlockSpec((LANES, LANES), lambda b, h, i: (0, 0)),
            pl.BlockSpec((V_ROWS, LANES), lambda b, h, i: (h, 0)),
        ],
        out_specs=pl.BlockSpec((tq, HEAD_DIM), lambda b, h, i: (b * nq + i, h)),
        out_shape=jax.ShapeDtypeStruct((rows, d), BF16),
        scratch_shapes=[pltpu.VMEM((HEAD_DIM + LANES, tq), BF16),
                        pltpu.VMEM((SCORE_SLOTS, tk, tk), F32),
                        pltpu.VMEM((1, tq), F32),
                        pltpu.VMEM((V_ROWS, tq), F32)],
        compiler_params=_params("parallel", "parallel", "arbitrary"),
        name="fox_attention",
    )(qt, auxqt, k, auxk, vt, km, auxkm, vtm)


def _oproj_ln_kernel(o_ref, x_ref, w_ref, gain_ref, bias_ref, y_ref, mix_sc, *, alpha):
    tm = x_ref.shape[0]
    sub = min(SUB_ROWS, tm)

    def matmul(rb):
        mix_sc[rb % 2] = jnp.dot(o_ref[rb * sub:(rb + 1) * sub, :], w_ref[...],
                                 preferred_element_type=F32)

    def epilogue(rb):
        rows = slice(rb * sub, (rb + 1) * sub)
        y = alpha * x_ref[rows, :].astype(F32) + mix_sc[rb % 2]
        y_ref[rows, :] = _layer_norm(y, gain_ref[...], bias_ref[...]).astype(y_ref.dtype)

    _sub_block_pipeline(tm // sub, matmul, epilogue)


def _oproj_ln(o, x, w, gain, bias, *, alpha, tm):
    rows, d = x.shape
    const = lambda i: (0, 0)
    return pl.pallas_call(
        functools.partial(_oproj_ln_kernel, alpha=alpha),
        grid=(rows // tm,),
        in_specs=[pl.BlockSpec((tm, d), lambda i: (i, 0)),
                  pl.BlockSpec((tm, d), lambda i: (i, 0)),
                  pl.BlockSpec(w.shape, const),
                  pl.BlockSpec((1, d), const),
                  pl.BlockSpec((1, d), const)],
        out_specs=pl.BlockSpec((tm, d), lambda i: (i, 0)),
        out_shape=jax.ShapeDtypeStruct((rows, d), ACT_DTYPE),
        scratch_shapes=[pltpu.VMEM((2, min(SUB_ROWS, tm), d), F32)],
        compiler_params=_params("parallel"),
        name="oproj_ln",
    )(o, x, w, gain, bias)


def kernel(x, meta, ffn1_wg, ffn1_wu, ffn1_wd, ffn2_wg, ffn2_wu, ffn2_wd, ln_gain, ln_bias,
           conv_w_in, conv_w, conv_w_out, kv_w, f_bias, attn_w_q, attn_w_o):
    batch, seq, d = x.shape
    n_meta = meta.shape[0]
    depth = ffn1_wg.shape[0]
    n_heads = f_bias.shape[0]
    assert depth == 2 and conv_w_in.shape[0] == 1 and attn_w_q.shape[0] == 1
    assert d == n_heads * HEAD_DIM and n_heads == HEAD_GROUP
    assert n_meta <= LANES and n_meta % 16 == 0
    alpha = (2 * depth) ** 0.25
    q_scale = LOG2E / math.sqrt(HEAD_DIM)
    attn_tile = 512

    bf = lambda w: w.astype(BF16)
    gain = lambda l, i: ln_gain[l, i].reshape(1, d).astype(F32)
    bias = lambda l, i: ln_bias[l, i].reshape(1, d).astype(F32)
    ffn_w = {1: (ffn1_wg, ffn1_wu, ffn1_wd), 2: (ffn2_wg, ffn2_wu, ffn2_wd)}

    def ffn(h, side, which, l, out_dtype=ACT_DTYPE):
        ln_idx = 0 if which == 1 else 2
        return _ffn_ln(h, side, *ffn_w[which], l, gain(l, ln_idx), bias(l, ln_idx),
                       alpha=alpha, tm=1024, out_dtype=out_dtype)

    xs = x.reshape(batch * seq, d)
    ms = meta.astype(x.dtype)

    w_in, w_out, cw = bf(conv_w_in[0]), bf(conv_w_out[0]), conv_w[0].astype(F32)
    conv = functools.partial(_conv_ln, alpha=alpha, tm=2048)
    h1, m1 = ffn(xs, ms, 1, 0)
    zero_carry = jnp.zeros((CONV_CARRY_ROWS, d), F32)
    m2, m_tail = conv(m1, w_in, cw, w_out, gain(0, 1), bias(0, 1), zero_carry, rows_per_seq=n_meta)
    h2, _ = conv(h1, w_in, cw, w_out, gain(0, 1), bias(0, 1), m_tail, rows_per_seq=seq)
    h3, m3 = ffn(h2, m2, 2, 0)

    w_kv = bf(kv_w[:, :2 * d])
    w_f = jnp.pad(bf(kv_w[:, 2 * d:]), ((0, 0), (0, LANES - n_heads)))
    fb = jnp.pad(f_bias.astype(F32), (0, LANES - n_heads)).reshape(1, LANES)
    kv = functools.partial(_shared_kv, tm=2 * attn_tile, n_heads=n_heads)
    km, vm, cm, _, auxkm = kv(m3, w_kv, w_f, fb, jnp.zeros((1, LANES), F32),
                              rows_per_seq=n_meta, vt_block=None)
    k, vt, _, auxq, auxk = kv(h3, w_kv, w_f, fb, cm[n_meta - 1:n_meta],
                              rows_per_seq=seq, vt_block=attn_tile)
    pad_aux = np.zeros((LANES - n_meta, LANES), np.float32)
    pad_aux[:, AUX_K_LANE:AUX_K_LANE + n_heads] = -MASK_VALUE
    km = jnp.pad(km, ((0, LANES - n_meta), (0, 0)))
    auxkm = jnp.concatenate([auxkm, jnp.asarray(pad_aux, BF16)], axis=0)
    ones_tile = np.zeros((n_heads, BF16_SUBLANES, LANES), np.float32)
    ones_tile[:, 0, :] = 1.0
    vtm = jnp.pad(vm.T, ((0, 0), (0, LANES - n_meta))).reshape(n_heads, HEAD_DIM, LANES)
    vtm = jnp.concatenate([vtm, jnp.asarray(ones_tile, BF16)], axis=1).reshape(n_heads * V_ROWS, LANES)

    h4, _ = ffn(h3, None, 1, 1)
    q = _qproj(h4, bf(attn_w_q[0]), scale=q_scale, tm=2048)
    o = _attention(q, auxq, k, auxk, vt, km, auxkm, vtm, batch=batch, seq=seq, n_heads=n_heads,
                   tq=8 * attn_tile, tk=attn_tile)
    h5 = _oproj_ln(o, h4, bf(attn_w_o[0]), gain(1, 1), bias(1, 1), alpha=alpha, tm=2048)
    out, _ = ffn(h5, None, 2, 1, out_dtype=x.dtype)
    return out.reshape(batch, seq, d)
```

```python
import functools
import math

import jax
import jax.numpy as jnp
import numpy as np
from jax import lax
from jax.experimental import pallas as pl
from jax.experimental.pallas import tpu as pltpu

F32 = jnp.float32
BF16 = jnp.bfloat16

LN_EPS = 1e-5
MASK_VALUE = 1e30
LOG2E = math.log2(math.e)
HEAD_DIM = 128
LANES = 128
CONV_CARRY_ROWS = 8
FFN_CHUNK = 256
SUB_ROWS = 256
ACT_DTYPE = BF16
CONV_PROJ_LOOKAHEAD = 2
CONV_PROJ_SLOTS = CONV_PROJ_LOOKAHEAD + 1
FFN_ROWS = 1024
CONV_ROWS = 1024
KV_ROWS = 1024
PROJ_ROWS = 2048
ATTN_KEY_BLOCK = 512
ATTN_QUERY_TILE = 4096
VMEM_LIMIT_BYTES = 56 * 1024 * 1024

N_SPLIT = 3
HEAD_GROUP = 8
AUX_Q_LANE = 0
AUX_K_LANE = N_SPLIT * HEAD_GROUP
QK_LOOKAHEAD = 2
SCORE_SLOTS = QK_LOOKAHEAD + 2
KV_UNROLL = 4
BF16_SUBLANES = 16
V_ROWS = HEAD_DIM + BF16_SUBLANES


def _ones_row_tile(width):
    row = lax.broadcasted_iota(jnp.int32, (BF16_SUBLANES, width), 0)
    return jnp.where(row == 0, 1.0, 0.0).astype(BF16)


def _params(*semantics):
    return pltpu.CompilerParams(dimension_semantics=semantics, vmem_limit_bytes=VMEM_LIMIT_BYTES)


def _layer_norm(y, gain, bias):
    mu = jnp.mean(y, axis=-1, keepdims=True)
    yc = y - mu
    var = jnp.mean(yc * yc, axis=-1, keepdims=True)
    return yc * lax.rsqrt(var + LN_EPS) * gain + bias


def _ffn_ln_kernel(*refs, alpha, n_chunks, n_side):
    if n_side:
        (x_ref, side_ref, wg_ref, wu_ref, wd_ref, gain_ref, bias_ref, o_ref, oside_ref,
         wgu_sc, wd_sc, xb_sc, gu_sc, a_sc, y_sc, aside_sc) = refs
    else:
        (x_ref, wg_ref, wu_ref, wd_ref, gain_ref, bias_ref, o_ref,
         wgu_sc, wd_sc, xb_sc, gu_sc, a_sc, y_sc) = refs
    i = pl.program_id(0)
    tf = wg_ref.shape[1]

    @pl.when(i < n_chunks)
    def _stage_weights():
        wgu_sc[i, :, 0:tf] = wg_ref[...].astype(BF16)
        wgu_sc[i, :, tf:2 * tf] = wu_ref[...].astype(BF16)
        wd_sc[pl.ds(pl.multiple_of(i * tf, tf), tf), :] = (0.5 * wd_ref[...]).astype(BF16)
        if n_side:
            gu = jnp.dot(side_ref[...].astype(BF16), wgu_sc[i], preferred_element_type=F32)
            g, u = gu[:, 0:tf], gu[:, tf:2 * tf]
            aside_sc[i] = (g * jax.nn.sigmoid(g) * u).astype(BF16)

    if n_side:
        @pl.when(i == n_chunks - 1)
        def _side_rows_finish():
            half_ffn = jnp.dot(aside_sc[0], wd_sc[0:tf, :], preferred_element_type=F32)
            for c in range(1, n_chunks):
                half_ffn += jnp.dot(aside_sc[c], wd_sc[c * tf:(c + 1) * tf, :],
                                    preferred_element_type=F32)
            y = alpha * side_ref[...].astype(F32) + half_ffn
            oside_ref[...] = _layer_norm(y, gain_ref[...], bias_ref[...]).astype(oside_ref.dtype)

    def ffn_rows(src_ref, dst_ref, r):
        if src_ref.dtype == BF16:
            lhs_ref = src_ref
        else:
            xb_sc[0:r, :] = src_ref[...].astype(BF16)
            lhs_ref = xb_sc.at[0:r, :]

        def gate_up(c):
            gu_sc[c % 2, 0:r, :] = jnp.dot(lhs_ref[...], wgu_sc[c], preferred_element_type=F32)

        gate_up(0)
        for c in range(n_chunks):
            if c + 1 < n_chunks:
                gate_up(c + 1)
            g = gu_sc[c % 2, 0:r, 0:tf]
            u = gu_sc[c % 2, 0:r, tf:2 * tf]
            a_sc[0:r, c * tf:(c + 1) * tf] = (g * jax.nn.sigmoid(g) * u).astype(BF16)
        sub = min(SUB_ROWS, r)

        def down(rb):
            y_sc[rb % 2, 0:sub, :] = jnp.dot(a_sc[rb * sub:(rb + 1) * sub, :], wd_sc[...],
                                             preferred_element_type=F32)

        down(0)
        for rb in range(r // sub):
            if (rb + 1) * sub < r:
                down(rb + 1)
            rows = slice(rb * sub, (rb + 1) * sub)
            y = alpha * src_ref[rows, :].astype(F32) + y_sc[rb % 2, 0:sub, :]
            dst_ref[rows, :] = _layer_norm(y, gain_ref[...], bias_ref[...]).astype(dst_ref.dtype)

    @pl.when(i >= n_chunks)
    def _row_tile():
        ffn_rows(x_ref, o_ref, x_ref.shape[0])


def _ffn_ln(x, side, wg, wu, wd, layer, gain, bias, *, alpha, tm, out_dtype):
    rows, d = x.shape
    dff = wg.shape[2]
    tf = FFN_CHUNK
    n_chunks = dff // tf
    n_side = 0 if side is None else side.shape[0]
    chunk = lambda i: jnp.minimum(i, n_chunks - 1)
    tile = lambda i: (jnp.maximum(i - n_chunks, 0), 0)
    const = lambda i: (0, 0)
    in_specs = [pl.BlockSpec((tm, d), tile)]
    out_specs = [pl.BlockSpec((tm, d), tile)]
    out_shape = [jax.ShapeDtypeStruct((rows, d), out_dtype)]
    operands = [x]
    if n_side:
        in_specs.append(pl.BlockSpec((n_side, d), const))
        out_specs.append(pl.BlockSpec((n_side, d), const))
        out_shape.append(jax.ShapeDtypeStruct((n_side, d), out_dtype))
        operands.append(side)
    in_specs += [
        pl.BlockSpec((None, d, tf), lambda i: (layer, 0, chunk(i))),
        pl.BlockSpec((None, d, tf), lambda i: (layer, 0, chunk(i))),
        pl.BlockSpec((None, tf, d), lambda i: (layer, chunk(i), 0)),
        pl.BlockSpec((1, d), const),
        pl.BlockSpec((1, d), const),
    ]
    scratch_shapes = [pltpu.VMEM((n_chunks, d, 2 * tf), BF16), pltpu.VMEM((dff, d), BF16),
                      pltpu.VMEM((tm, d), BF16), pltpu.VMEM((2, tm, 2 * tf), F32),
                      pltpu.VMEM((tm, dff), BF16), pltpu.VMEM((2, SUB_ROWS, d), F32)]
    if n_side:
        scratch_shapes.append(pltpu.VMEM((n_chunks, n_side, tf), BF16))
    outs = pl.pallas_call(
        functools.partial(_ffn_ln_kernel, alpha=alpha, n_chunks=n_chunks, n_side=n_side),
        grid=(n_chunks + rows // tm,),
        in_specs=in_specs,
        out_specs=out_specs,
        out_shape=out_shape,
        scratch_shapes=scratch_shapes,
        compiler_params=_params("arbitrary"),
        name="ffn_ln",
    )(*operands, wg, wu, wd, gain, bias)
    return outs if n_side else (outs[0], None)


def _conv_ln_kernel(x_ref, win_ref, cw_ref, wout_ref, gain_ref, bias_ref, carry_ref,
                    o_ref, tail_ref, ubuf, proj_sc, z_sc, mix_sc, *, alpha, tiles_per_seq):
    i = pl.program_id(0)
    tm, d = x_ref.shape
    c = CONV_CARRY_ROWS
    sub = min(SUB_ROWS, tm)
    n_sub = tm // sub

    @pl.when(i % tiles_per_seq == 0)
    def _():
        ubuf[0:c, :] = carry_ref[...]

    cw = cw_ref[...]

    def in_proj(rb):
        xb = x_ref[rb * sub:(rb + 1) * sub, :].astype(BF16)
        proj_sc[rb % CONV_PROJ_SLOTS] = jnp.dot(xb, win_ref[...], preferred_element_type=F32)

    def gated_conv(rb):
        slot, base = rb % CONV_PROJ_SLOTS, c + rb * sub
        u = proj_sc[slot, :, d:2 * d] * proj_sc[slot, :, 2 * d:3 * d]
        ubuf[base:base + sub, :] = u
        y = (cw[2:3, :] * u
             + cw[1:2, :] * ubuf[base - 1:base - 1 + sub, :]
             + cw[0:1, :] * ubuf[base - 2:base - 2 + sub, :])
        z_sc[rb % 2] = (proj_sc[slot, :, 0:d] * y).astype(BF16)

    def out_proj(rb):
        mix_sc[rb % 2] = jnp.dot(z_sc[rb % 2], wout_ref[...], preferred_element_type=F32)

    def finish(rb):
        rows = slice(rb * sub, (rb + 1) * sub)
        y = alpha * x_ref[rows, :].astype(F32) + mix_sc[rb % 2]
        o_ref[rows, :] = _layer_norm(y, gain_ref[...], bias_ref[...]).astype(o_ref.dtype)

    for rb in range(min(CONV_PROJ_LOOKAHEAD, n_sub)):
        in_proj(rb)
    for rb in range(n_sub):
        gated_conv(rb)
        if rb + CONV_PROJ_LOOKAHEAD < n_sub:
            in_proj(rb + CONV_PROJ_LOOKAHEAD)
        out_proj(rb)
        if rb >= 1:
            finish(rb - 1)
    finish(n_sub - 1)

    tail = ubuf[tm:tm + c, :]
    ubuf[0:c, :] = tail
    tail_ref[...] = tail


def _conv_ln(x, w_in, conv_w, w_out, gain, bias, carry, *, alpha, tm, rows_per_seq):
    rows, d = x.shape
    tm = min(tm, rows_per_seq)
    const = lambda i: (0, 0)
    return pl.pallas_call(
        functools.partial(_conv_ln_kernel, alpha=alpha, tiles_per_seq=rows_per_seq // tm),
        grid=(rows // tm,),
        in_specs=[
            pl.BlockSpec((tm, d), lambda i: (i, 0)),
            pl.BlockSpec(w_in.shape, const),
            pl.BlockSpec(conv_w.shape, const),
            pl.BlockSpec(w_out.shape, const),
            pl.BlockSpec((1, d), const),
            pl.BlockSpec((1, d), const),
            pl.BlockSpec((CONV_CARRY_ROWS, d), const),
        ],
        out_specs=[pl.BlockSpec((tm, d), lambda i: (i, 0)),
                   pl.BlockSpec((CONV_CARRY_ROWS, d), const)],
        out_shape=[jax.ShapeDtypeStruct((rows, d), ACT_DTYPE),
                   jax.ShapeDtypeStruct((CONV_CARRY_ROWS, d), F32)],
        scratch_shapes=[pltpu.VMEM((tm + CONV_CARRY_ROWS, d), F32),
                        pltpu.VMEM((CONV_PROJ_SLOTS, min(SUB_ROWS, tm), 3 * d), F32),
                        pltpu.VMEM((2, min(SUB_ROWS, tm), d), BF16),
                        pltpu.VMEM((2, min(SUB_ROWS, tm), d), F32)],
        compiler_params=_params("arbitrary"),
        name="conv_ln",
    )(x, w_in, conv_w, w_out, gain, bias, carry)


def _kv_kernel(x_ref, wkv_ref, wf_ref, fb_ref, cin_ref, k_ref, v_ref, c_ref, auxq_ref, auxk_ref,
               carry_sc, kv_sc, *, tiles_per_seq, n_heads, transpose_v):
    i = pl.program_id(0)
    tm, d = x_ref.shape
    sub = min(SUB_ROWS, tm)
    n_sub = tm // sub

    @pl.when(i % tiles_per_seq == 0)
    def _():
        carry_sc[...] = cin_ref[...]

    def kv_matmul(rb):
        xb = x_ref[rb * sub:(rb + 1) * sub, :].astype(BF16)
        kv_sc[rb % 2] = jnp.dot(xb, wkv_ref[...], preferred_element_type=F32)

    def kv_store(rb):
        rows = slice(rb * sub, (rb + 1) * sub)
        k_ref[rows, :] = kv_sc[rb % 2, :, 0:d].astype(BF16)
        if transpose_v:
            vt = kv_sc[rb % 2, :, d:2 * d].T.astype(BF16)
            vt_block = v_ref.shape[2]
            blk, col = divmod(rb * sub, vt_block)
            for hh in range(n_heads):
                v_ref[blk, hh * V_ROWS:hh * V_ROWS + HEAD_DIM, col:col + sub] = (
                    vt[hh * HEAD_DIM:(hh + 1) * HEAD_DIM, :])
        else:
            v_ref[rows, :] = kv_sc[rb % 2, :, d:2 * d].astype(BF16)

    f = jnp.dot(x_ref[...].astype(BF16), wf_ref[...], preferred_element_type=F32) + fb_ref[...]
    kv_matmul(0)
    if transpose_v:
        for blk in range(v_ref.shape[0]):
            for hh in range(n_heads):
                v_ref[blk, hh * V_ROWS + HEAD_DIM:(hh + 1) * V_ROWS, :] = _ones_row_tile(v_ref.shape[2])
    log_f = jnp.minimum(f, 0.0) - jnp.log1p(jnp.exp(-jnp.abs(f)))
    row = lax.broadcasted_iota(jnp.int32, log_f.shape, 0)
    csum = log_f
    shift = 1
    while shift < tm:
        csum = csum + jnp.where(row >= shift, pltpu.roll(csum, shift, axis=0), 0.0)
        shift *= 2
    csum = csum + carry_sc[...]
    carry_sc[...] = csum[tm - 1:tm, :]
    c_ref[...] = csum

    lane = lax.broadcasted_iota(jnp.int32, csum.shape, 1)
    c2 = jnp.where(lane < n_heads, csum * LOG2E, 0.0)
    hi = c2.astype(BF16).astype(F32)
    r1 = c2 - hi
    mid = r1.astype(BF16).astype(F32)
    lo = r1 - mid
    aux = hi + pltpu.roll(mid, HEAD_GROUP, axis=1) + pltpu.roll(lo, 2 * HEAD_GROUP, axis=1)
    auxq_ref[...] = (aux.T if transpose_v else aux).astype(BF16)
    auxk_ref[...] = (-pltpu.roll(aux, AUX_K_LANE - AUX_Q_LANE, axis=1)).astype(BF16)

    for rb in range(n_sub):
        if rb + 1 < n_sub:
            kv_matmul(rb + 1)
        kv_store(rb)


def _shared_kv(x, w_kv, w_f, f_bias, c_in, *, tm, rows_per_seq, n_heads, vt_block):
    rows, d = x.shape
    tm = min(tm, rows_per_seq)
    transpose_v = vt_block is not None
    const = lambda i: (0, 0)
    row_tile = lambda i: (i, 0)
    if transpose_v:
        v_spec = pl.BlockSpec((tm // vt_block, n_heads * V_ROWS, vt_block), lambda i: (i, 0, 0))
        v_shape = jax.ShapeDtypeStruct((rows // vt_block, n_heads * V_ROWS, vt_block), BF16)
        auxq_spec = pl.BlockSpec((LANES, tm), lambda i: (0, i))
        auxq_shape = jax.ShapeDtypeStruct((LANES, rows), BF16)
    else:
        v_spec = pl.BlockSpec((tm, d), row_tile)
        v_shape = jax.ShapeDtypeStruct((rows, d), BF16)
        auxq_spec = pl.BlockSpec((tm, LANES), row_tile)
        auxq_shape = jax.ShapeDtypeStruct((rows, LANES), BF16)
    return pl.pallas_call(
        functools.partial(_kv_kernel, tiles_per_seq=rows_per_seq // tm, n_heads=n_heads,
                          transpose_v=transpose_v),
        grid=(rows // tm,),
        in_specs=[
            pl.BlockSpec((tm, d), row_tile),
            pl.BlockSpec(w_kv.shape, const),
            pl.BlockSpec(w_f.shape, const),
            pl.BlockSpec((1, LANES), const),
            pl.BlockSpec((1, LANES), const),
        ],
        out_specs=[pl.BlockSpec((tm, d), row_tile), v_spec,
                   pl.BlockSpec((tm, LANES), row_tile),
                   auxq_spec,
                   pl.BlockSpec((tm, LANES), row_tile)],
        out_shape=[jax.ShapeDtypeStruct((rows, d), BF16), v_shape,
                   jax.ShapeDtypeStruct((rows, LANES), F32),
                   auxq_shape,
                   jax.ShapeDtypeStruct((rows, LANES), BF16)],
        scratch_shapes=[pltpu.VMEM((1, LANES), F32),
                        pltpu.VMEM((2, min(SUB_ROWS, tm), 2 * d), F32)],
        compiler_params=_params("arbitrary"),
        name="shared_kv",
    )(x, w_kv, w_f, f_bias, c_in)


def _sub_block_pipeline(n_sub, matmul, epilogue):
    matmul(0)
    for rb in range(n_sub):
        if rb + 1 < n_sub:
            matmul(rb + 1)
        epilogue(rb)


def _qproj_kernel(x_ref, w_ref, qt_ref, y_sc, *, scale):
    tm = x_ref.shape[0]
    sub = min(SUB_ROWS, tm)

    def matmul(rb):
        xb = x_ref[rb * sub:(rb + 1) * sub, :].astype(BF16)
        y_sc[rb % 2] = jnp.dot(xb, w_ref[...], preferred_element_type=F32)

    def epilogue(rb):
        qt_ref[:, rb * sub:(rb + 1) * sub] = (y_sc[rb % 2] * scale).T.astype(qt_ref.dtype)

    _sub_block_pipeline(tm // sub, matmul, epilogue)


def _qproj(x, w, *, scale, tm):
    rows, d = x.shape
    dq = w.shape[1]
    return pl.pallas_call(
        functools.partial(_qproj_kernel, scale=scale),
        grid=(rows // tm,),
        in_specs=[pl.BlockSpec((tm, d), lambda i: (i, 0)),
                  pl.BlockSpec(w.shape, lambda i: (0, 0))],
        out_specs=pl.BlockSpec((dq, tm), lambda i: (0, i)),
        out_shape=jax.ShapeDtypeStruct((dq, rows), BF16),
        scratch_shapes=[pltpu.VMEM((2, min(SUB_ROWS, tm), dq), F32)],
        compiler_params=_params("parallel"),
        name="qproj",
    )(x, w)


def _head_one_hot(h, base, axis):
    shape = (1, LANES) if axis == 1 else (LANES, 1)
    pos = lax.broadcasted_iota(jnp.int32, shape, axis)
    hit = (pos == base + h) | (pos == base + HEAD_GROUP + h) | (pos == base + 2 * HEAD_GROUP + h)
    return jnp.where(hit, 1.0, 0.0).astype(BF16)


def _attn_kernel(qt_ref, auxqt_ref, k_ref, auxk_ref, vt_ref, km_ref, auxkm_ref, vtm_ref, o_ref,
                 qa_sc, s_sc, m_sc, acc_sc, *, tk):
    h = pl.program_id(1)
    qi = pl.program_id(2)
    tq = qt_ref.shape[1]
    ts = tk
    n_strips = tq // ts

    q_sel = _head_one_hot(h, AUX_K_LANE, 0)
    k_sel = _head_one_hot(h, AUX_Q_LANE, 1)
    qa_sc[0:HEAD_DIM, :] = qt_ref[...]
    qa_sc[HEAD_DIM:, :] = auxqt_ref[...] + q_sel
    m_sc[...] = jnp.full(m_sc.shape, -MASK_VALUE, F32)
    acc_sc[...] = jnp.zeros(acc_sc.shape, F32)

    def strip_scores(t, ka, c, causal):
        cs = slice(c * ts, (c + 1) * ts)
        s = jnp.dot(ka, qa_sc[:, cs], preferred_element_type=F32)
        if causal:
            key = lax.broadcasted_iota(jnp.int32, s.shape, 0)
            qry = lax.broadcasted_iota(jnp.int32, s.shape, 1)
            s = jnp.where(key <= qry, s, -MASK_VALUE)
        s_sc[t % SCORE_SLOTS, 0:s.shape[0], :] = s
        return jnp.max(s, axis=0, keepdims=True)

    def strip_update(t, s_max, vt, c):
        cs = slice(c * ts, (c + 1) * ts)
        m = m_sc[:, cs]
        m_new = jnp.maximum(m, s_max)
        corr = jnp.exp2(m - m_new)
        m_sc[:, cs] = m_new
        p = jnp.exp2(s_sc[t % SCORE_SLOTS, 0:vt.shape[1], :] - m_new)
        pv = jnp.dot(vt, p.astype(BF16), preferred_element_type=F32)
        acc_sc[:, cs] = corr * acc_sc[:, cs] + pv

    def run_tasks(tasks):
        pending = [strip_scores(t, ka, c, causal)
                   for t, (ka, _, c, causal) in enumerate(tasks[:QK_LOOKAHEAD])]
        for t, (_, vt, c, _) in enumerate(tasks):
            if t + QK_LOOKAHEAD < len(tasks):
                ka_n, _, c_n, causal_n = tasks[t + QK_LOOKAHEAD]
                pending.append(strip_scores(t + QK_LOOKAHEAD, ka_n, c_n, causal_n))
            strip_update(t, pending.pop(0), vt, c)

    def key_block(j):
        off = pl.multiple_of(j * tk, tk)
        ka = jnp.concatenate([k_ref[pl.ds(off, tk), :], auxk_ref[pl.ds(off, tk), :] + k_sel], axis=1)
        return ka, vt_ref[j]

    def full_blocks(jj, _):
        blocks = [key_block(jj * KV_UNROLL + u) for u in range(KV_UNROLL)]
        run_tasks([(ka, vt, c, False) for ka, vt in blocks for c in range(n_strips)])
        return 0

    assert n_strips % KV_UNROLL == 0
    lax.fori_loop(0, qi * (n_strips // KV_UNROLL), full_blocks, 0)

    diag = [key_block(qi * n_strips + d) for d in range(n_strips)]
    ka_m = jnp.concatenate([km_ref[...], auxkm_ref[...] + k_sel], axis=1)
    run_tasks([(diag[d][0], diag[d][1], c, c == d) for d in range(n_strips) for c in range(d, n_strips)]
              + [(ka_m, vtm_ref[...], c, False) for c in range(n_strips)])

    denom = acc_sc[HEAD_DIM:HEAD_DIM + 1, :]
    o_ref[...] = (acc_sc[0:HEAD_DIM, :] / denom).T.astype(o_ref.dtype)


def _attention(qt, auxqt, k, auxk, vt, km, auxkm, vtm, *, batch, seq, n_heads, tq, tk):
    d, rows = qt.shape
    nq = seq // tq
    nk = seq // tk
    assert vt.shape == (batch * nk, n_heads * V_ROWS, tk)
    return pl.pallas_call(
        functools.partial(_attn_kernel, tk=tk),
        grid=(batch, n_heads, nq),
        in_specs=[
            pl.BlockSpec((HEAD_DIM, tq), lambda b, h, i: (h, b * nq + i)),
            pl.BlockSpec((LANES, tq), lambda b, h, i: (0, b * nq + i)),
            pl.BlockSpec((seq, HEAD_DIM), lambda b, h, i: (b, h)),
            pl.BlockSpec((seq, LANES), lambda b, h, i: (b, 0)),
            pl.BlockSpec((nk, V_ROWS, tk), lambda b, h, i: (b, h, 0)),
            pl.BlockSpec((LANES, HEAD_DIM), lambda b, h, i: (0, h)),
            pl.BlockSpec((LANES, LANES), lambda b, h, i: (0, 0)),
            pl.BlockSpec((V_ROWS, LANES), lambda b, h, i: (h, 0)),
        ],
        out_specs=pl.BlockSpec((tq, HEAD_DIM), lambda b, h, i: (b * nq + i, h)),
        out_shape=jax.ShapeDtypeStruct((rows, d), BF16),
        scratch_shapes=[pltpu.VMEM((HEAD_DIM + LANES, tq), BF16),
                        pltpu.VMEM((SCORE_SLOTS, tk, tk), F32),
                        pltpu.VMEM((1, tq), F32),
                        pltpu.VMEM((V_ROWS, tq), F32)],
        compiler_params=_params("parallel", "parallel", "arbitrary"),
        name="fox_attention",
    )(qt, auxqt, k, auxk, vt, km, auxkm, vtm)


def _oproj_ln_kernel(o_ref, x_ref, w_ref, gain_ref, bias_ref, y_ref, mix_sc, *, alpha):
    tm = x_ref.shape[0]
    sub = min(SUB_ROWS, tm)

    def matmul(rb):
        mix_sc[rb % 2] = jnp.dot(o_ref[rb * sub:(rb + 1) * sub, :], w_ref[...],
                                 preferred_element_type=F32)

    def epilogue(rb):
        rows = slice(rb * sub, (rb + 1) * sub)
        y = alpha * x_ref[rows, :].astype(F32) + mix_sc[rb % 2]
        y_ref[rows, :] = _layer_norm(y, gain_ref[...], bias_ref[...]).astype(y_ref.dtype)

    _sub_block_pipeline(tm // sub, matmul, epilogue)


def _oproj_ln(o, x, w, gain, bias, *, alpha, tm):
    rows, d = x.shape
    const = lambda i: (0, 0)
    return pl.pallas_call(
        functools.partial(_oproj_ln_kernel, alpha=alpha),
        grid=(rows // tm,),
        in_specs=[pl.BlockSpec((tm, d), lambda i: (i, 0)),
                  pl.BlockSpec((tm, d), lambda i: (i, 0)),
                  pl.BlockSpec(w.shape, const),
                  pl.BlockSpec((1, d), const),
                  pl.BlockSpec((1, d), const)],
        out_specs=pl.BlockSpec((tm, d), lambda i: (i, 0)),
        out_shape=jax.ShapeDtypeStruct((rows, d), ACT_DTYPE),
        scratch_shapes=[pltpu.VMEM((2, min(SUB_ROWS, tm), d), F32)],
        compiler_params=_params("parallel"),
        name="oproj_ln",
    )(o, x, w, gain, bias)


def kernel(x, meta, ffn1_wg, ffn1_wu, ffn1_wd, ffn2_wg, ffn2_wu, ffn2_wd, ln_gain, ln_bias,
           conv_w_in, conv_w, conv_w_out, kv_w, f_bias, attn_w_q, attn_w_o):
    batch, seq, d = x.shape
    n_meta = meta.shape[0]
    depth = ffn1_wg.shape[0]
    n_heads = f_bias.shape[0]
    assert depth == 2 and conv_w_in.shape[0] == 1 and attn_w_q.shape[0] == 1
    assert d == n_heads * HEAD_DIM and n_heads == HEAD_GROUP
    assert n_meta <= LANES and n_meta % 16 == 0
    alpha = (2 * depth) ** 0.25
    q_scale = LOG2E / math.sqrt(HEAD_DIM)
    assert (batch * seq) % max(FFN_ROWS, CONV_ROWS, KV_ROWS, PROJ_ROWS) == 0
    assert seq % ATTN_QUERY_TILE == 0 and seq % max(CONV_ROWS, KV_ROWS) == 0

    bf = lambda w: w.astype(BF16)
    gain = lambda l, i: ln_gain[l, i].reshape(1, d).astype(F32)
    bias = lambda l, i: ln_bias[l, i].reshape(1, d).astype(F32)
    ffn_w = {1: (ffn1_wg, ffn1_wu, ffn1_wd), 2: (ffn2_wg, ffn2_wu, ffn2_wd)}

    def ffn(h, side, which, l, out_dtype=ACT_DTYPE):
        ln_idx = 0 if which == 1 else 2
        return _ffn_ln(h, side, *ffn_w[which], l, gain(l, ln_idx), bias(l, ln_idx),
                       alpha=alpha, tm=FFN_ROWS, out_dtype=out_dtype)

    xs = x.reshape(batch * seq, d)
    ms = meta.astype(x.dtype)

    w_in, w_out, cw = bf(conv_w_in[0]), bf(conv_w_out[0]), conv_w[0].astype(F32)
    conv = functools.partial(_conv_ln, alpha=alpha, tm=CONV_ROWS)
    h1, m1 = ffn(xs, ms, 1, 0)
    zero_carry = jnp.zeros((CONV_CARRY_ROWS, d), F32)
    m2, m_tail = conv(m1, w_in, cw, w_out, gain(0, 1), bias(0, 1), zero_carry, rows_per_seq=n_meta)
    h2, _ = conv(h1, w_in, cw, w_out, gain(0, 1), bias(0, 1), m_tail, rows_per_seq=seq)
    h3, m3 = ffn(h2, m2, 2, 0)

    w_kv = bf(kv_w[:, :2 * d])
    w_f = jnp.pad(bf(kv_w[:, 2 * d:]), ((0, 0), (0, LANES - n_heads)))
    fb = jnp.pad(f_bias.astype(F32), (0, LANES - n_heads)).reshape(1, LANES)
    kv = functools.partial(_shared_kv, tm=KV_ROWS, n_heads=n_heads)
    km, vm, cm, _, auxkm = kv(m3, w_kv, w_f, fb, jnp.zeros((1, LANES), F32),
                              rows_per_seq=n_meta, vt_block=None)
    k, vt, _, auxq, auxk = kv(h3, w_kv, w_f, fb, cm[n_meta - 1:n_meta],
                              rows_per_seq=seq, vt_block=ATTN_KEY_BLOCK)
    pad_aux = np.zeros((LANES - n_meta, LANES), np.float32)
    pad_aux[:, AUX_K_LANE:AUX_K_LANE + n_heads] = -MASK_VALUE
    km = jnp.pad(km, ((0, LANES - n_meta), (0, 0)))
    auxkm = jnp.concatenate([auxkm, jnp.asarray(pad_aux, BF16)], axis=0)
    ones_tile = np.zeros((n_heads, BF16_SUBLANES, LANES), np.float32)
    ones_tile[:, 0, :] = 1.0
    vtm = jnp.pad(vm.T, ((0, 0), (0, LANES - n_meta))).reshape(n_heads, HEAD_DIM, LANES)
    vtm = jnp.concatenate([vtm, jnp.asarray(ones_tile, BF16)], axis=1).reshape(n_heads * V_ROWS, LANES)

    h4, _ = ffn(h3, None, 1, 1)
    q = _qproj(h4, bf(attn_w_q[0]), scale=q_scale, tm=PROJ_ROWS)
    o = _attention(q, auxq, k, auxk, vt, km, auxkm, vtm, batch=batch, seq=seq, n_heads=n_heads,
                   tq=ATTN_QUERY_TILE, tk=ATTN_KEY_BLOCK)
    h5 = _oproj_ln(o, h4, bf(attn_w_o[0]), gain(1, 1), bias(1, 1), alpha=alpha, tm=PROJ_ROWS)
    out, _ = ffn(h5, None, 2, 1, out_dtype=x.dtype)
    return out.reshape(batch, seq, d)
```

```python
import functools
import math

import jax
import jax.numpy as jnp
import numpy as np
from jax import lax
from jax.experimental import pallas as pl
from jax.experimental.pallas import tpu as pltpu

F32 = jnp.float32
BF16 = jnp.bfloat16

LN_EPS = 1e-5
MASK_VALUE = 1e30
LOG2E = math.log2(math.e)
HEAD_DIM = 128
LANES = 128
CONV_CARRY_ROWS = 8
FFN_CHUNK = 256
SUB_ROWS = 256
ACT_DTYPE = BF16
CONV_PROJ_LOOKAHEAD = 2
CONV_PROJ_SLOTS = CONV_PROJ_LOOKAHEAD + 1
FFN_ROWS = 1024
CONV_ROWS = 1024
KV_ROWS = 1024
PROJ_ROWS = 2048
ATTN_KEY_BLOCK = 512
ATTN_QUERY_TILE = 4096
VMEM_LIMIT_BYTES = 56 * 1024 * 1024

N_SPLIT = 3
HEAD_GROUP = 8
AUX_Q_LANE = 0
AUX_K_LANE = N_SPLIT * HEAD_GROUP
QK_LOOKAHEAD = 2
SCORE_SLOTS = QK_LOOKAHEAD + 2
KV_UNROLL = 4
BF16_SUBLANES = 16
V_ROWS = HEAD_DIM + BF16_SUBLANES


def _ones_row_tile(width):
    row = lax.broadcasted_iota(jnp.int32, (BF16_SUBLANES, width), 0)
    return jnp.where(row == 0, 1.0, 0.0).astype(BF16)


def _params(*semantics):
    return pltpu.CompilerParams(dimension_semantics=semantics, vmem_limit_bytes=VMEM_LIMIT_BYTES)


def _layer_norm(y, gain, bias):
    mu = jnp.mean(y, axis=-1, keepdims=True)
    yc = y - mu
    var = jnp.mean(yc * yc, axis=-1, keepdims=True)
    return yc * lax.rsqrt(var + LN_EPS) * gain + bias


def _ffn_ln_kernel(*refs, alpha, n_chunks, n_side, q_scale):
    if q_scale is not None:
        (x_ref, wg_ref, wu_ref, wd_ref, gain_ref, bias_ref, wq_ref, o_ref, qt_ref,
         wgu_sc, wd_sc, xb_sc, gu_sc, a_sc, y_sc, q_sc) = refs
    elif n_side:
        (x_ref, side_ref, wg_ref, wu_ref, wd_ref, gain_ref, bias_ref, o_ref, oside_ref,
         wgu_sc, wd_sc, xb_sc, gu_sc, a_sc, y_sc, aside_sc) = refs
    else:
        (x_ref, wg_ref, wu_ref, wd_ref, gain_ref, bias_ref, o_ref,
         wgu_sc, wd_sc, xb_sc, gu_sc, a_sc, y_sc) = refs
    i = pl.program_id(0)
    tf = wg_ref.shape[1]

    @pl.when(i < n_chunks)
    def _stage_weights():
        wgu_sc[i, :, 0:tf] = wg_ref[...].astype(BF16)
        wgu_sc[i, :, tf:2 * tf] = wu_ref[...].astype(BF16)
        wd_sc[pl.ds(pl.multiple_of(i * tf, tf), tf), :] = (0.5 * wd_ref[...]).astype(BF16)
        if n_side:
            gu = jnp.dot(side_ref[...].astype(BF16), wgu_sc[i], preferred_element_type=F32)
            g, u = gu[:, 0:tf], gu[:, tf:2 * tf]
            aside_sc[i] = (g * jax.nn.sigmoid(g) * u).astype(BF16)

    if n_side:
        @pl.when(i == n_chunks - 1)
        def _side_rows_finish():
            half_ffn = jnp.dot(aside_sc[0], wd_sc[0:tf, :], preferred_element_type=F32)
            for c in range(1, n_chunks):
                half_ffn += jnp.dot(aside_sc[c], wd_sc[c * tf:(c + 1) * tf, :],
                                    preferred_element_type=F32)
            y = alpha * side_ref[...].astype(F32) + half_ffn
            oside_ref[...] = _layer_norm(y, gain_ref[...], bias_ref[...]).astype(oside_ref.dtype)

    def ffn_rows(src_ref, dst_ref, r):
        if src_ref.dtype == BF16:
            lhs_ref = src_ref
        else:
            xb_sc[0:r, :] = src_ref[...].astype(BF16)
            lhs_ref = xb_sc.at[0:r, :]

        def gate_up(c):
            gu_sc[c % 2, 0:r, :] = jnp.dot(lhs_ref[...], wgu_sc[c], preferred_element_type=F32)

        gate_up(0)
        for c in range(n_chunks):
            if c + 1 < n_chunks:
                gate_up(c + 1)
            g = gu_sc[c % 2, 0:r, 0:tf]
            u = gu_sc[c % 2, 0:r, tf:2 * tf]
            a_sc[0:r, c * tf:(c + 1) * tf] = (g * jax.nn.sigmoid(g) * u).astype(BF16)
        sub = min(SUB_ROWS, r)

        def down(rb):
            y_sc[rb % 2, 0:sub, :] = jnp.dot(a_sc[rb * sub:(rb + 1) * sub, :], wd_sc[...],
                                             preferred_element_type=F32)

        def emit_q(rb):
            qt_ref[:, rb * sub:(rb + 1) * sub] = (q_sc[rb % 2] * q_scale).T.astype(qt_ref.dtype)

        down(0)
        for rb in range(r // sub):
            if (rb + 1) * sub < r:
                down(rb + 1)
            rows = slice(rb * sub, (rb + 1) * sub)
            y = alpha * src_ref[rows, :].astype(F32) + y_sc[rb % 2, 0:sub, :]
            h = _layer_norm(y, gain_ref[...], bias_ref[...]).astype(dst_ref.dtype)
            dst_ref[rows, :] = h
            if q_scale is not None:
                if rb >= 1:
                    emit_q(rb - 1)
                q_sc[rb % 2] = jnp.dot(h.astype(BF16), wq_ref[...], preferred_element_type=F32)
        if q_scale is not None:
            emit_q(r // sub - 1)

    @pl.when(i >= n_chunks)
    def _row_tile():
        ffn_rows(x_ref, o_ref, x_ref.shape[0])


def _ffn_ln(x, side, wg, wu, wd, layer, gain, bias, *, alpha, tm, out_dtype, wq=None, q_scale=None):
    assert (wq is None) == (q_scale is None) and (wq is None or side is None)
    rows, d = x.shape
    dff = wg.shape[2]
    tf = FFN_CHUNK
    n_chunks = dff // tf
    n_side = 0 if side is None else side.shape[0]
    chunk = lambda i: jnp.minimum(i, n_chunks - 1)
    tile = lambda i: (jnp.maximum(i - n_chunks, 0), 0)
    const = lambda i: (0, 0)
    in_specs = [pl.BlockSpec((tm, d), tile)]
    out_specs = [pl.BlockSpec((tm, d), tile)]
    out_shape = [jax.ShapeDtypeStruct((rows, d), out_dtype)]
    operands = [x]
    if n_side:
        in_specs.append(pl.BlockSpec((n_side, d), const))
        out_specs.append(pl.BlockSpec((n_side, d), const))
        out_shape.append(jax.ShapeDtypeStruct((n_side, d), out_dtype))
        operands.append(side)
    in_specs += [
        pl.BlockSpec((None, d, tf), lambda i: (layer, 0, chunk(i))),
        pl.BlockSpec((None, d, tf), lambda i: (layer, 0, chunk(i))),
        pl.BlockSpec((None, tf, d), lambda i: (layer, chunk(i), 0)),
        pl.BlockSpec((1, d), const),
        pl.BlockSpec((1, d), const),
    ]
    xb_rows = BF16_SUBLANES if x.dtype == BF16 else tm
    scratch_shapes = [pltpu.VMEM((n_chunks, d, 2 * tf), BF16), pltpu.VMEM((dff, d), BF16),
                      pltpu.VMEM((xb_rows, d), BF16), pltpu.VMEM((2, tm, 2 * tf), F32),
                      pltpu.VMEM((tm, dff), BF16), pltpu.VMEM((2, SUB_ROWS, d), F32)]
    if n_side:
        scratch_shapes.append(pltpu.VMEM((n_chunks, n_side, tf), BF16))
    if wq is not None:
        dq = wq.shape[1]
        in_specs.append(pl.BlockSpec(wq.shape, const))
        out_specs.append(pl.BlockSpec((dq, tm), lambda i: (0, jnp.maximum(i - n_chunks, 0))))
        out_shape.append(jax.ShapeDtypeStruct((dq, rows), BF16))
        scratch_shapes.append(pltpu.VMEM((2, SUB_ROWS, dq), F32))
        extra = (wq,)
    else:
        extra = ()
    outs = pl.pallas_call(
        functools.partial(_ffn_ln_kernel, alpha=alpha, n_chunks=n_chunks, n_side=n_side,
                          q_scale=q_scale),
        grid=(n_chunks + rows // tm,),
        in_specs=in_specs,
        out_specs=out_specs,
        out_shape=out_shape,
        scratch_shapes=scratch_shapes,
        compiler_params=_params("arbitrary"),
        name="ffn_ln",
    )(*operands, wg, wu, wd, gain, bias, *extra)
    return outs if (n_side or wq is not None) else (outs[0], None)


def _conv_ln_kernel(x_ref, win_ref, cw_ref, wout_ref, gain_ref, bias_ref, carry_ref,
                    o_ref, tail_ref, ubuf, proj_sc, z_sc, mix_sc, *, alpha, tiles_per_seq):
    i = pl.program_id(0)
    tm, d = x_ref.shape
    c = CONV_CARRY_ROWS
    sub = min(SUB_ROWS, tm)
    n_sub = tm // sub

    @pl.when(i % tiles_per_seq == 0)
    def _():
        ubuf[0:c, :] = carry_ref[...]

    cw = cw_ref[...]

    def in_proj(rb):
        xb = x_ref[rb * sub:(rb + 1) * sub, :].astype(BF16)
        proj_sc[rb % CONV_PROJ_SLOTS] = jnp.dot(xb, win_ref[...], preferred_element_type=F32)

    def gated_conv(rb):
        slot, base = rb % CONV_PROJ_SLOTS, c + rb * sub
        u = proj_sc[slot, :, d:2 * d] * proj_sc[slot, :, 2 * d:3 * d]
        ubuf[base:base + sub, :] = u
        y = (cw[2:3, :] * u
             + cw[1:2, :] * ubuf[base - 1:base - 1 + sub, :]
             + cw[0:1, :] * ubuf[base - 2:base - 2 + sub, :])
        z_sc[rb % 2] = (proj_sc[slot, :, 0:d] * y).astype(BF16)

    def out_proj(rb):
        mix_sc[rb % 2] = jnp.dot(z_sc[rb % 2], wout_ref[...], preferred_element_type=F32)

    def finish(rb):
        rows = slice(rb * sub, (rb + 1) * sub)
        y = alpha * x_ref[rows, :].astype(F32) + mix_sc[rb % 2]
        o_ref[rows, :] = _layer_norm(y, gain_ref[...], bias_ref[...]).astype(o_ref.dtype)

    for rb in range(min(CONV_PROJ_LOOKAHEAD, n_sub)):
        in_proj(rb)
    for rb in range(n_sub):
        gated_conv(rb)
        if rb + CONV_PROJ_LOOKAHEAD < n_sub:
            in_proj(rb + CONV_PROJ_LOOKAHEAD)
        out_proj(rb)
        if rb >= 1:
            finish(rb - 1)
    finish(n_sub - 1)

    tail = ubuf[tm:tm + c, :]
    ubuf[0:c, :] = tail
    tail_ref[...] = tail


def _conv_ln(x, w_in, conv_w, w_out, gain, bias, carry, *, alpha, tm, rows_per_seq):
    rows, d = x.shape
    tm = min(tm, rows_per_seq)
    const = lambda i: (0, 0)
    return pl.pallas_call(
        functools.partial(_conv_ln_kernel, alpha=alpha, tiles_per_seq=rows_per_seq // tm),
        grid=(rows // tm,),
        in_specs=[
            pl.BlockSpec((tm, d), lambda i: (i, 0)),
            pl.BlockSpec(w_in.shape, const),
            pl.BlockSpec(conv_w.shape, const),
            pl.BlockSpec(w_out.shape, const),
            pl.BlockSpec((1, d), const),
            pl.BlockSpec((1, d), const),
            pl.BlockSpec((CONV_CARRY_ROWS, d), const),
        ],
        out_specs=[pl.BlockSpec((tm, d), lambda i: (i, 0)),
                   pl.BlockSpec((CONV_CARRY_ROWS, d), const)],
        out_shape=[jax.ShapeDtypeStruct((rows, d), ACT_DTYPE),
                   jax.ShapeDtypeStruct((CONV_CARRY_ROWS, d), F32)],
        scratch_shapes=[pltpu.VMEM((tm + CONV_CARRY_ROWS, d), F32),
                        pltpu.VMEM((CONV_PROJ_SLOTS, min(SUB_ROWS, tm), 3 * d), F32),
                        pltpu.VMEM((2, min(SUB_ROWS, tm), d), BF16),
                        pltpu.VMEM((2, min(SUB_ROWS, tm), d), F32)],
        compiler_params=_params("arbitrary"),
        name="conv_ln",
    )(x, w_in, conv_w, w_out, gain, bias, carry)


def _kv_kernel(x_ref, wkv_ref, wf_ref, fb_ref, cin_ref, k_ref, v_ref, c_ref, auxq_ref, auxk_ref,
               carry_sc, kv_sc, *, tiles_per_seq, n_heads, transpose_v):
    i = pl.program_id(0)
    tm, d = x_ref.shape
    sub = min(SUB_ROWS, tm)
    n_sub = tm // sub

    @pl.when(i % tiles_per_seq == 0)
    def _():
        carry_sc[...] = cin_ref[...]

    def kv_matmul(rb):
        xb = x_ref[rb * sub:(rb + 1) * sub, :].astype(BF16)
        kv_sc[rb % 2] = jnp.dot(xb, wkv_ref[...], preferred_element_type=F32)

    def kv_store(rb):
        rows = slice(rb * sub, (rb + 1) * sub)
        k_ref[rows, :] = kv_sc[rb % 2, :, 0:d].astype(BF16)
        if transpose_v:
            vt = kv_sc[rb % 2, :, d:2 * d].T.astype(BF16)
            vt_block = v_ref.shape[2]
            blk, col = divmod(rb * sub, vt_block)
            for hh in range(n_heads):
                v_ref[blk, hh * V_ROWS:hh * V_ROWS + HEAD_DIM, col:col + sub] = (
                    vt[hh * HEAD_DIM:(hh + 1) * HEAD_DIM, :])
        else:
            v_ref[rows, :] = kv_sc[rb % 2, :, d:2 * d].astype(BF16)

    f = jnp.dot(x_ref[...].astype(BF16), wf_ref[...], preferred_element_type=F32) + fb_ref[...]
    kv_matmul(0)
    if transpose_v:
        for blk in range(v_ref.shape[0]):
            for hh in range(n_heads):
                v_ref[blk, hh * V_ROWS + HEAD_DIM:(hh + 1) * V_ROWS, :] = _ones_row_tile(v_ref.shape[2])
    log_f = jnp.minimum(f, 0.0) - jnp.log1p(jnp.exp(-jnp.abs(f)))
    row = lax.broadcasted_iota(jnp.int32, log_f.shape, 0)
    csum = log_f
    shift = 1
    while shift < tm:
        csum = csum + jnp.where(row >= shift, pltpu.roll(csum, shift, axis=0), 0.0)
        shift *= 2
    csum = csum + carry_sc[...]
    carry_sc[...] = csum[tm - 1:tm, :]
    c_ref[...] = csum

    lane = lax.broadcasted_iota(jnp.int32, csum.shape, 1)
    c2 = jnp.where(lane < n_heads, csum * LOG2E, 0.0)
    hi = c2.astype(BF16).astype(F32)
    r1 = c2 - hi
    mid = r1.astype(BF16).astype(F32)
    lo = r1 - mid
    aux = hi + pltpu.roll(mid, HEAD_GROUP, axis=1) + pltpu.roll(lo, 2 * HEAD_GROUP, axis=1)
    auxq_ref[...] = (aux.T if transpose_v else aux).astype(BF16)
    auxk_ref[...] = (-pltpu.roll(aux, AUX_K_LANE - AUX_Q_LANE, axis=1)).astype(BF16)

    for rb in range(n_sub):
        if rb + 1 < n_sub:
            kv_matmul(rb + 1)
        kv_store(rb)


def _shared_kv(x, w_kv, w_f, f_bias, c_in, *, tm, rows_per_seq, n_heads, vt_block):
    rows, d = x.shape
    tm = min(tm, rows_per_seq)
    transpose_v = vt_block is not None
    const = lambda i: (0, 0)
    row_tile = lambda i: (i, 0)
    if transpose_v:
        v_spec = pl.BlockSpec((tm // vt_block, n_heads * V_ROWS, vt_block), lambda i: (i, 0, 0))
        v_shape = jax.ShapeDtypeStruct((rows // vt_block, n_heads * V_ROWS, vt_block), BF16)
        auxq_spec = pl.BlockSpec((LANES, tm), lambda i: (0, i))
        auxq_shape = jax.ShapeDtypeStruct((LANES, rows), BF16)
    else:
        v_spec = pl.BlockSpec((tm, d), row_tile)
        v_shape = jax.ShapeDtypeStruct((rows, d), BF16)
        auxq_spec = pl.BlockSpec((tm, LANES), row_tile)
        auxq_shape = jax.ShapeDtypeStruct((rows, LANES), BF16)
    return pl.pallas_call(
        functools.partial(_kv_kernel, tiles_per_seq=rows_per_seq // tm, n_heads=n_heads,
                          transpose_v=transpose_v),
        grid=(rows // tm,),
        in_specs=[
            pl.BlockSpec((tm, d), row_tile),
            pl.BlockSpec(w_kv.shape, const),
            pl.BlockSpec(w_f.shape, const),
            pl.BlockSpec((1, LANES), const),
            pl.BlockSpec((1, LANES), const),
        ],
        out_specs=[pl.BlockSpec((tm, d), row_tile), v_spec,
                   pl.BlockSpec((tm, LANES), row_tile),
                   auxq_spec,
                   pl.BlockSpec((tm, LANES), row_tile)],
        out_shape=[jax.ShapeDtypeStruct((rows, d), BF16), v_shape,
                   jax.ShapeDtypeStruct((rows, LANES), F32),
                   auxq_shape,
                   jax.ShapeDtypeStruct((rows, LANES), BF16)],
        scratch_shapes=[pltpu.VMEM((1, LANES), F32),
                        pltpu.VMEM((2, min(SUB_ROWS, tm), 2 * d), F32)],
        compiler_params=_params("arbitrary"),
        name="shared_kv",
    )(x, w_kv, w_f, f_bias, c_in)


def _sub_block_pipeline(n_sub, matmul, epilogue):
    matmul(0)
    for rb in range(n_sub):
        if rb + 1 < n_sub:
            matmul(rb + 1)
        epilogue(rb)


def _head_one_hot(h, base, axis):
    shape = (1, LANES) if axis == 1 else (LANES, 1)
    pos = lax.broadcasted_iota(jnp.int32, shape, axis)
    hit = (pos == base + h) | (pos == base + HEAD_GROUP + h) | (pos == base + 2 * HEAD_GROUP + h)
    return jnp.where(hit, 1.0, 0.0).astype(BF16)


def _attn_kernel(qt_ref, auxqt_ref, k_ref, auxk_ref, vt_ref, km_ref, auxkm_ref, vtm_ref, o_ref,
                 qa_sc, s_sc, m_sc, acc_sc, *, tk):
    h = pl.program_id(1)
    qi = pl.program_id(2)
    tq = qt_ref.shape[1]
    ts = tk
    n_strips = tq // ts

    q_sel = _head_one_hot(h, AUX_K_LANE, 0)
    k_sel = _head_one_hot(h, AUX_Q_LANE, 1)
    qa_sc[0:HEAD_DIM, :] = qt_ref[...]
    qa_sc[HEAD_DIM:, :] = auxqt_ref[...] + q_sel
    m_sc[...] = jnp.full(m_sc.shape, -MASK_VALUE, F32)
    acc_sc[...] = jnp.zeros(acc_sc.shape, F32)

    def strip_scores(t, ka, c, causal):
        cs = slice(c * ts, (c + 1) * ts)
        s = jnp.dot(ka, qa_sc[:, cs], preferred_element_type=F32)
        if causal:
            key = lax.broadcasted_iota(jnp.int32, s.shape, 0)
            qry = lax.broadcasted_iota(jnp.int32, s.shape, 1)
            s = jnp.where(key <= qry, s, -MASK_VALUE)
        s_sc[t % SCORE_SLOTS, 0:s.shape[0], :] = s
        return jnp.max(s, axis=0, keepdims=True)

    def strip_update(t, s_max, vt, c):
        cs = slice(c * ts, (c + 1) * ts)
        m = m_sc[:, cs]
        m_new = jnp.maximum(m, s_max)
        corr = jnp.exp2(m - m_new)
        m_sc[:, cs] = m_new
        p = jnp.exp2(s_sc[t % SCORE_SLOTS, 0:vt.shape[1], :] - m_new)
        pv = jnp.dot(vt, p.astype(BF16), preferred_element_type=F32)
        acc_sc[:, cs] = corr * acc_sc[:, cs] + pv

    def run_tasks(tasks):
        pending = [strip_scores(t, ka, c, causal)
                   for t, (ka, _, c, causal) in enumerate(tasks[:QK_LOOKAHEAD])]
        for t, (_, vt, c, _) in enumerate(tasks):
            if t + QK_LOOKAHEAD < len(tasks):
                ka_n, _, c_n, causal_n = tasks[t + QK_LOOKAHEAD]
                pending.append(strip_scores(t + QK_LOOKAHEAD, ka_n, c_n, causal_n))
            strip_update(t, pending.pop(0), vt, c)

    def key_block(j):
        off = pl.multiple_of(j * tk, tk)
        ka = jnp.concatenate([k_ref[pl.ds(off, tk), :], auxk_ref[pl.ds(off, tk), :] + k_sel], axis=1)
        return ka, vt_ref[j]

    def full_blocks(jj, _):
        blocks = [key_block(jj * KV_UNROLL + u) for u in range(KV_UNROLL)]
        run_tasks([(ka, vt, c, False) for ka, vt in blocks for c in range(n_strips)])
        return 0

    assert n_strips % KV_UNROLL == 0
    lax.fori_loop(0, qi * (n_strips // KV_UNROLL), full_blocks, 0)

    diag = [key_block(qi * n_strips + d) for d in range(n_strips)]
    ka_m = jnp.concatenate([km_ref[...], auxkm_ref[...] + k_sel], axis=1)
    run_tasks([(diag[d][0], diag[d][1], c, c == d) for d in range(n_strips) for c in range(d, n_strips)]
              + [(ka_m, vtm_ref[...], c, False) for c in range(n_strips)])

    denom = acc_sc[HEAD_DIM:HEAD_DIM + 1, :]
    o_ref[...] = (acc_sc[0:HEAD_DIM, :] / denom).T.astype(o_ref.dtype)


def _attention(qt, auxqt, k, auxk, vt, km, auxkm, vtm, *, batch, seq, n_heads, tq, tk):
    d, rows = qt.shape
    nq = seq // tq
    nk = seq // tk
    assert vt.shape == (batch * nk, n_heads * V_ROWS, tk)
    return pl.pallas_call(
        functools.partial(_attn_kernel, tk=tk),
        grid=(batch, n_heads, nq),
        in_specs=[
            pl.BlockSpec((HEAD_DIM, tq), lambda b, h, i: (h, b * nq + i)),
            pl.BlockSpec((LANES, tq), lambda b, h, i: (0, b * nq + i)),
            pl.BlockSpec((seq, HEAD_DIM), lambda b, h, i: (b, h)),
            pl.BlockSpec((seq, LANES), lambda b, h, i: (b, 0)),
            pl.BlockSpec((nk, V_ROWS, tk), lambda b, h, i: (b, h, 0)),
            pl.BlockSpec((LANES, HEAD_DIM), lambda b, h, i: (0, h)),
            pl.BlockSpec((LANES, LANES), lambda b, h, i: (0, 0)),
            pl.BlockSpec((V_ROWS, LANES), lambda b, h, i: (h, 0)),
        ],
        out_specs=pl.BlockSpec((tq, HEAD_DIM), lambda b, h, i: (b * nq + i, h)),
        out_shape=jax.ShapeDtypeStruct((rows, d), BF16),
        scratch_shapes=[pltpu.VMEM((HEAD_DIM + LANES, tq), BF16),
                        pltpu.VMEM((SCORE_SLOTS, tk, tk), F32),
                        pltpu.VMEM((1, tq), F32),
                        pltpu.VMEM((V_ROWS, tq), F32)],
        compiler_params=_params("parallel", "parallel", "arbitrary"),
        name="fox_attention",
    )(qt, auxqt, k, auxk, vt, km, auxkm, vtm)


def _oproj_ln_kernel(o_ref, x_ref, w_ref, gain_ref, bias_ref, y_ref, mix_sc, *, alpha):
    tm = x_ref.shape[0]
    sub = min(SUB_ROWS, tm)

    def matmul(rb):
        mix_sc[rb % 2] = jnp.dot(o_ref[rb * sub:(rb + 1) * sub, :], w_ref[...],
                                 preferred_element_type=F32)

    def epilogue(rb):
        rows = slice(rb * sub, (rb + 1) * sub)
        y = alpha * x_ref[rows, :].astype(F32) + mix_sc[rb % 2]
        y_ref[rows, :] = _layer_norm(y, gain_ref[...], bias_ref[...]).astype(y_ref.dtype)

    _sub_block_pipeline(tm // sub, matmul, epilogue)


def _oproj_ln(o, x, w, gain, bias, *, alpha, tm):
    rows, d = x.shape
    const = lambda i: (0, 0)
    return pl.pallas_call(
        functools.partial(_oproj_ln_kernel, alpha=alpha),
        grid=(rows // tm,),
        in_specs=[pl.BlockSpec((tm, d), lambda i: (i, 0)),
                  pl.BlockSpec((tm, d), lambda i: (i, 0)),
                  pl.BlockSpec(w.shape, const),
                  pl.BlockSpec((1, d), const),
                  pl.BlockSpec((1, d), const)],
        out_specs=pl.BlockSpec((tm, d), lambda i: (i, 0)),
        out_shape=jax.ShapeDtypeStruct((rows, d), ACT_DTYPE),
        scratch_shapes=[pltpu.VMEM((2, min(SUB_ROWS, tm), d), F32)],
        compiler_params=_params("parallel"),
        name="oproj_ln",
    )(o, x, w, gain, bias)


def kernel(x, meta, ffn1_wg, ffn1_wu, ffn1_wd, ffn2_wg, ffn2_wu, ffn2_wd, ln_gain, ln_bias,
           conv_w_in, conv_w, conv_w_out, kv_w, f_bias, attn_w_q, attn_w_o):
    batch, seq, d = x.shape
    n_meta = meta.shape[0]
    depth = ffn1_wg.shape[0]
    n_heads = f_bias.shape[0]
    assert depth == 2 and conv_w_in.shape[0] == 1 and attn_w_q.shape[0] == 1
    assert d == n_heads * HEAD_DIM and n_heads == HEAD_GROUP
    assert n_meta <= LANES and n_meta % 16 == 0
    alpha = (2 * depth) ** 0.25
    q_scale = LOG2E / math.sqrt(HEAD_DIM)
    assert (batch * seq) % max(FFN_ROWS, CONV_ROWS, KV_ROWS, PROJ_ROWS) == 0
    assert seq % ATTN_QUERY_TILE == 0 and seq % max(CONV_ROWS, KV_ROWS) == 0

    bf = lambda w: w.astype(BF16)
    gain = lambda l, i: ln_gain[l, i].reshape(1, d).astype(F32)
    bias = lambda l, i: ln_bias[l, i].reshape(1, d).astype(F32)
    ffn_w = {1: (ffn1_wg, ffn1_wu, ffn1_wd), 2: (ffn2_wg, ffn2_wu, ffn2_wd)}

    def ffn(h, side, which, l, out_dtype=ACT_DTYPE, **fused_q):
        ln_idx = 0 if which == 1 else 2
        return _ffn_ln(h, side, *ffn_w[which], l, gain(l, ln_idx), bias(l, ln_idx),
                       alpha=alpha, tm=FFN_ROWS, out_dtype=out_dtype, **fused_q)

    xs = x.reshape(batch * seq, d)
    ms = meta.astype(x.dtype)

    w_in, w_out, cw = bf(conv_w_in[0]), bf(conv_w_out[0]), conv_w[0].astype(F32)
    conv = functools.partial(_conv_ln, alpha=alpha, tm=CONV_ROWS)
    h1, m1 = ffn(xs, ms, 1, 0)
    zero_carry = jnp.zeros((CONV_CARRY_ROWS, d), F32)
    m2, m_tail = conv(m1, w_in, cw, w_out, gain(0, 1), bias(0, 1), zero_carry, rows_per_seq=n_meta)
    h2, _ = conv(h1, w_in, cw, w_out, gain(0, 1), bias(0, 1), m_tail, rows_per_seq=seq)
    h3, m3 = ffn(h2, m2, 2, 0)

    w_kv = bf(kv_w[:, :2 * d])
    w_f = jnp.pad(bf(kv_w[:, 2 * d:]), ((0, 0), (0, LANES - n_heads)))
    fb = jnp.pad(f_bias.astype(F32), (0, LANES - n_heads)).reshape(1, LANES)
    kv = functools.partial(_shared_kv, tm=KV_ROWS, n_heads=n_heads)
    km, vm, cm, _, auxkm = kv(m3, w_kv, w_f, fb, jnp.zeros((1, LANES), F32),
                              rows_per_seq=n_meta, vt_block=None)
    k, vt, _, auxq, auxk = kv(h3, w_kv, w_f, fb, cm[n_meta - 1:n_meta],
                              rows_per_seq=seq, vt_block=ATTN_KEY_BLOCK)
    pad_aux = np.zeros((LANES - n_meta, LANES), np.float32)
    pad_aux[:, AUX_K_LANE:AUX_K_LANE + n_heads] = -MASK_VALUE
    km = jnp.pad(km, ((0, LANES - n_meta), (0, 0)))
    auxkm = jnp.concatenate([auxkm, jnp.asarray(pad_aux, BF16)], axis=0)
    ones_tile = np.zeros((n_heads, BF16_SUBLANES, LANES), np.float32)
    ones_tile[:, 0, :] = 1.0
    vtm = jnp.pad(vm.T, ((0, 0), (0, LANES - n_meta))).reshape(n_heads, HEAD_DIM, LANES)
    vtm = jnp.concatenate([vtm, jnp.asarray(ones_tile, BF16)], axis=1).reshape(n_heads * V_ROWS, LANES)

    h4, q = ffn(h3, None, 1, 1, wq=bf(attn_w_q[0]), q_scale=q_scale)
    o = _attention(q, auxq, k, auxk, vt, km, auxkm, vtm, batch=batch, seq=seq, n_heads=n_heads,
                   tq=ATTN_QUERY_TILE, tk=ATTN_KEY_BLOCK)
    h5 = _oproj_ln(o, h4, bf(attn_w_o[0]), gain(1, 1), bias(1, 1), alpha=alpha, tm=PROJ_ROWS)
    out, _ = ffn(h5, None, 2, 1, out_dtype=x.dtype)
    return out.reshape(batch, seq, d)
```

```python
import functools
import math

import jax
import jax.numpy as jnp
import numpy as np
from jax import lax
from jax.experimental import pallas as pl
from jax.experimental.pallas import tpu as pltpu

F32 = jnp.float32
BF16 = jnp.bfloat16

LN_EPS = 1e-5
MASK_VALUE = 1e30
LOG2E = math.log2(math.e)
HEAD_DIM = 128
LANES = 128
CONV_CARRY_ROWS = 8
FFN_CHUNK = 256
SUB_ROWS = 256
ACT_DTYPE = BF16
CONV_PROJ_LOOKAHEAD = 2
CONV_PROJ_SLOTS = CONV_PROJ_LOOKAHEAD + 1
FFN_ROWS = 1024
CONV_ROWS = 1024
KV_ROWS = 1024
PROJ_ROWS = 2048
ATTN_KEY_BLOCK = 512
ATTN_QUERY_TILE = 4096
VMEM_LIMIT_BYTES = 56 * 1024 * 1024

N_SPLIT = 3
HEAD_GROUP = 8
AUX_Q_LANE = 0
AUX_K_LANE = N_SPLIT * HEAD_GROUP
QK_LOOKAHEAD = 2
SCORE_SLOTS = QK_LOOKAHEAD + 2
KV_UNROLL = 4
BF16_SUBLANES = 16
V_ROWS = HEAD_DIM + BF16_SUBLANES


def _ones_row_tile(width):
    row = lax.broadcasted_iota(jnp.int32, (BF16_SUBLANES, width), 0)
    return jnp.where(row == 0, 1.0, 0.0).astype(BF16)


def _params(*semantics):
    return pltpu.CompilerParams(dimension_semantics=semantics, vmem_limit_bytes=VMEM_LIMIT_BYTES)


def _layer_norm(y, gain, bias, alpha):
    mu = jnp.mean(y, axis=-1, keepdims=True)
    yc = y - mu
    var = jnp.mean(yc * yc, axis=-1, keepdims=True)
    return yc * lax.rsqrt(var + LN_EPS / (alpha * alpha)) * gain + bias


def _ffn_ln_kernel(*refs, alpha, n_chunks, n_side, q_scale):
    if q_scale is not None:
        (x_ref, wg_ref, wu_ref, wd_ref, gain_ref, bias_ref, wq_ref, o_ref, qt_ref,
         wgu_sc, wd_sc, xb_sc, gu_sc, a_sc, y_sc, q_sc) = refs
    elif n_side:
        (x_ref, side_ref, wg_ref, wu_ref, wd_ref, gain_ref, bias_ref, o_ref, oside_ref,
         wgu_sc, wd_sc, xb_sc, gu_sc, a_sc, y_sc, aside_sc) = refs
    else:
        (x_ref, wg_ref, wu_ref, wd_ref, gain_ref, bias_ref, o_ref,
         wgu_sc, wd_sc, xb_sc, gu_sc, a_sc, y_sc) = refs
    i = pl.program_id(0)
    tf = wg_ref.shape[1]

    @pl.when(i < n_chunks)
    def _stage_weights():
        wgu_sc[i, :, 0:tf] = wg_ref[...].astype(BF16)
        wgu_sc[i, :, tf:2 * tf] = wu_ref[...].astype(BF16)
        wd_sc[pl.ds(pl.multiple_of(i * tf, tf), tf), :] = ((0.5 / alpha) * wd_ref[...]).astype(BF16)
        if n_side:
            gu = jnp.dot(side_ref[...].astype(BF16), wgu_sc[i], preferred_element_type=F32)
            g, u = gu[:, 0:tf], gu[:, tf:2 * tf]
            aside_sc[i] = (g * jax.nn.sigmoid(g) * u).astype(BF16)

    if n_side:
        @pl.when(i == n_chunks - 1)
        def _side_rows_finish():
            half_ffn = jnp.dot(aside_sc[0], wd_sc[0:tf, :], preferred_element_type=F32)
            for c in range(1, n_chunks):
                half_ffn += jnp.dot(aside_sc[c], wd_sc[c * tf:(c + 1) * tf, :],
                                    preferred_element_type=F32)
            y = side_ref[...].astype(F32) + half_ffn
            oside_ref[...] = _layer_norm(y, gain_ref[...], bias_ref[...], alpha).astype(oside_ref.dtype)

    def ffn_rows(src_ref, dst_ref, r):
        if src_ref.dtype == BF16:
            lhs_ref = src_ref
        else:
            xb_sc[0:r, :] = src_ref[...].astype(BF16)
            lhs_ref = xb_sc.at[0:r, :]

        def gate_up(c):
            gu_sc[c % 2, 0:r, :] = jnp.dot(lhs_ref[...], wgu_sc[c], preferred_element_type=F32)

        gate_up(0)
        for c in range(n_chunks):
            if c + 1 < n_chunks:
                gate_up(c + 1)
            g = gu_sc[c % 2, 0:r, 0:tf]
            u = gu_sc[c % 2, 0:r, tf:2 * tf]
            a_sc[0:r, c * tf:(c + 1) * tf] = (g * jax.nn.sigmoid(g) * u).astype(BF16)
        sub = min(SUB_ROWS, r)

        def down(rb):
            y_sc[rb % 2, 0:sub, :] = jnp.dot(a_sc[rb * sub:(rb + 1) * sub, :], wd_sc[...],
                                             preferred_element_type=F32)

        def emit_q(rb):
            qt_ref[:, rb * sub:(rb + 1) * sub] = (q_sc[rb % 2] * q_scale).T.astype(qt_ref.dtype)

        down(0)
        for rb in range(r // sub):
            if (rb + 1) * sub < r:
                down(rb + 1)
            rows = slice(rb * sub, (rb + 1) * sub)
            y = src_ref[rows, :].astype(F32) + y_sc[rb % 2, 0:sub, :]
            h = _layer_norm(y, gain_ref[...], bias_ref[...], alpha).astype(dst_ref.dtype)
            dst_ref[rows, :] = h
            if q_scale is not None:
                if rb >= 1:
                    emit_q(rb - 1)
                q_sc[rb % 2] = jnp.dot(h.astype(BF16), wq_ref[...], preferred_element_type=F32)
        if q_scale is not None:
            emit_q(r // sub - 1)

    @pl.when(i >= n_chunks)
    def _row_tile():
        ffn_rows(x_ref, o_ref, x_ref.shape[0])


def _ffn_ln(x, side, wg, wu, wd, layer, gain, bias, *, alpha, tm, out_dtype, wq=None, q_scale=None):
    assert (wq is None) == (q_scale is None) and (wq is None or side is None)
    rows, d = x.shape
    dff = wg.shape[2]
    tf = FFN_CHUNK
    n_chunks = dff // tf
    n_side = 0 if side is None else side.shape[0]
    chunk = lambda i: jnp.minimum(i, n_chunks - 1)
    tile = lambda i: (jnp.maximum(i - n_chunks, 0), 0)
    const = lambda i: (0, 0)
    in_specs = [pl.BlockSpec((tm, d), tile)]
    out_specs = [pl.BlockSpec((tm, d), tile)]
    out_shape = [jax.ShapeDtypeStruct((rows, d), out_dtype)]
    operands = [x]
    if n_side:
        in_specs.append(pl.BlockSpec((n_side, d), const))
        out_specs.append(pl.BlockSpec((n_side, d), const))
        out_shape.append(jax.ShapeDtypeStruct((n_side, d), out_dtype))
        operands.append(side)
    in_specs += [
        pl.BlockSpec((None, d, tf), lambda i: (layer, 0, chunk(i))),
        pl.BlockSpec((None, d, tf), lambda i: (layer, 0, chunk(i))),
        pl.BlockSpec((None, tf, d), lambda i: (layer, chunk(i), 0)),
        pl.BlockSpec((1, d), const),
        pl.BlockSpec((1, d), const),
    ]
    xb_rows = BF16_SUBLANES if x.dtype == BF16 else tm
    scratch_shapes = [pltpu.VMEM((n_chunks, d, 2 * tf), BF16), pltpu.VMEM((dff, d), BF16),
                      pltpu.VMEM((xb_rows, d), BF16), pltpu.VMEM((2, tm, 2 * tf), F32),
                      pltpu.VMEM((tm, dff), BF16), pltpu.VMEM((2, SUB_ROWS, d), F32)]
    if n_side:
        scratch_shapes.append(pltpu.VMEM((n_chunks, n_side, tf), BF16))
    if wq is not None:
        dq = wq.shape[1]
        in_specs.append(pl.BlockSpec(wq.shape, const))
        out_specs.append(pl.BlockSpec((dq, tm), lambda i: (0, jnp.maximum(i - n_chunks, 0))))
        out_shape.append(jax.ShapeDtypeStruct((dq, rows), BF16))
        scratch_shapes.append(pltpu.VMEM((2, SUB_ROWS, dq), F32))
        extra = (wq,)
    else:
        extra = ()
    outs = pl.pallas_call(
        functools.partial(_ffn_ln_kernel, alpha=alpha, n_chunks=n_chunks, n_side=n_side,
                          q_scale=q_scale),
        grid=(n_chunks + rows // tm,),
        in_specs=in_specs,
        out_specs=out_specs,
        out_shape=out_shape,
        scratch_shapes=scratch_shapes,
        compiler_params=_params("arbitrary"),
        name="ffn_ln",
    )(*operands, wg, wu, wd, gain, bias, *extra)
    return outs if (n_side or wq is not None) else (outs[0], None)


def _conv_ln_kernel(x_ref, win_ref, cw_ref, wout_ref, gain_ref, bias_ref, carry_ref,
                    o_ref, tail_ref, ubuf, proj_sc, z_sc, mix_sc, *, alpha, tiles_per_seq):
    i = pl.program_id(0)
    tm, d = x_ref.shape
    c = CONV_CARRY_ROWS
    sub = min(SUB_ROWS, tm)
    n_sub = tm // sub

    @pl.when(i % tiles_per_seq == 0)
    def _():
        ubuf[0:c, :] = carry_ref[...]

    cw = cw_ref[...]

    def in_proj(rb):
        xb = x_ref[rb * sub:(rb + 1) * sub, :].astype(BF16)
        proj_sc[rb % CONV_PROJ_SLOTS] = jnp.dot(xb, win_ref[...], preferred_element_type=F32)

    def gated_conv(rb):
        slot, base = rb % CONV_PROJ_SLOTS, c + rb * sub
        u = proj_sc[slot, :, d:2 * d] * proj_sc[slot, :, 2 * d:3 * d]
        ubuf[base:base + sub, :] = u
        y = (cw[2:3, :] * u
             + cw[1:2, :] * ubuf[base - 1:base - 1 + sub, :]
             + cw[0:1, :] * ubuf[base - 2:base - 2 + sub, :])
        z_sc[rb % 2] = (proj_sc[slot, :, 0:d] * y).astype(BF16)

    def out_proj(rb):
        mix_sc[rb % 2] = jnp.dot(z_sc[rb % 2], wout_ref[...], preferred_element_type=F32)

    def finish(rb):
        rows = slice(rb * sub, (rb + 1) * sub)
        y = x_ref[rows, :].astype(F32) + mix_sc[rb % 2]
        o_ref[rows, :] = _layer_norm(y, gain_ref[...], bias_ref[...], alpha).astype(o_ref.dtype)

    for rb in range(min(CONV_PROJ_LOOKAHEAD, n_sub)):
        in_proj(rb)
    for rb in range(n_sub):
        gated_conv(rb)
        if rb + CONV_PROJ_LOOKAHEAD < n_sub:
            in_proj(rb + CONV_PROJ_LOOKAHEAD)
        out_proj(rb)
        if rb >= 1:
            finish(rb - 1)
    finish(n_sub - 1)

    tail = ubuf[tm:tm + c, :]
    ubuf[0:c, :] = tail
    tail_ref[...] = tail


def _conv_ln(x, w_in, conv_w, w_out, gain, bias, carry, *, alpha, tm, rows_per_seq):
    rows, d = x.shape
    tm = min(tm, rows_per_seq)
    const = lambda i: (0, 0)
    return pl.pallas_call(
        functools.partial(_conv_ln_kernel, alpha=alpha, tiles_per_seq=rows_per_seq // tm),
        grid=(rows // tm,),
        in_specs=[
            pl.BlockSpec((tm, d), lambda i: (i, 0)),
            pl.BlockSpec(w_in.shape, const),
            pl.BlockSpec(conv_w.shape, const),
            pl.BlockSpec(w_out.shape, const),
            pl.BlockSpec((1, d), const),
            pl.BlockSpec((1, d), const),
            pl.BlockSpec((CONV_CARRY_ROWS, d), const),
        ],
        out_specs=[pl.BlockSpec((tm, d), lambda i: (i, 0)),
                   pl.BlockSpec((CONV_CARRY_ROWS, d), const)],
        out_shape=[jax.ShapeDtypeStruct((rows, d), ACT_DTYPE),
                   jax.ShapeDtypeStruct((CONV_CARRY_ROWS, d), F32)],
        scratch_shapes=[pltpu.VMEM((tm + CONV_CARRY_ROWS, d), F32),
                        pltpu.VMEM((CONV_PROJ_SLOTS, min(SUB_ROWS, tm), 3 * d), F32),
                        pltpu.VMEM((2, min(SUB_ROWS, tm), d), BF16),
                        pltpu.VMEM((2, min(SUB_ROWS, tm), d), F32)],
        compiler_params=_params("arbitrary"),
        name="conv_ln",
    )(x, w_in, conv_w, w_out, gain, bias, carry)


def _kv_kernel(x_ref, wkv_ref, wf_ref, fb_ref, cin_ref, k_ref, v_ref, c_ref, auxq_ref, auxk_ref,
               carry_sc, kv_sc, *, tiles_per_seq, n_heads, transpose_v):
    i = pl.program_id(0)
    tm, d = x_ref.shape
    sub = min(SUB_ROWS, tm)
    n_sub = tm // sub

    @pl.when(i % tiles_per_seq == 0)
    def _():
        carry_sc[...] = cin_ref[...]

    def kv_matmul(rb):
        xb = x_ref[rb * sub:(rb + 1) * sub, :].astype(BF16)
        kv_sc[rb % 2] = jnp.dot(xb, wkv_ref[...], preferred_element_type=F32)

    def kv_store(rb):
        rows = slice(rb * sub, (rb + 1) * sub)
        k_ref[rows, :] = kv_sc[rb % 2, :, 0:d].astype(BF16)
        if transpose_v:
            vt = kv_sc[rb % 2, :, d:2 * d].T.astype(BF16)
            vt_block = v_ref.shape[2]
            blk, col = divmod(rb * sub, vt_block)
            for hh in range(n_heads):
                v_ref[blk, hh * V_ROWS:hh * V_ROWS + HEAD_DIM, col:col + sub] = (
                    vt[hh * HEAD_DIM:(hh + 1) * HEAD_DIM, :])
        else:
            v_ref[rows, :] = kv_sc[rb % 2, :, d:2 * d].astype(BF16)

    f = jnp.dot(x_ref[...].astype(BF16), wf_ref[...], preferred_element_type=F32) + fb_ref[...]
    kv_matmul(0)
    if transpose_v:
        for blk in range(v_ref.shape[0]):
            for hh in range(n_heads):
                v_ref[blk, hh * V_ROWS + HEAD_DIM:(hh + 1) * V_ROWS, :] = _ones_row_tile(v_ref.shape[2])
    log_f = jnp.minimum(f, 0.0) - jnp.log1p(jnp.exp(-jnp.abs(f)))
    row = lax.broadcasted_iota(jnp.int32, log_f.shape, 0)
    csum = log_f
    shift = 1
    while shift < tm:
        csum = csum + jnp.where(row >= shift, pltpu.roll(csum, shift, axis=0), 0.0)
        shift *= 2
    csum = csum + carry_sc[...]
    carry_sc[...] = csum[tm - 1:tm, :]
    c_ref[...] = csum

    lane = lax.broadcasted_iota(jnp.int32, csum.shape, 1)
    c2 = jnp.where(lane < n_heads, csum * LOG2E, 0.0)
    hi = c2.astype(BF16).astype(F32)
    r1 = c2 - hi
    mid = r1.astype(BF16).astype(F32)
    lo = r1 - mid
    aux = hi + pltpu.roll(mid, HEAD_GROUP, axis=1) + pltpu.roll(lo, 2 * HEAD_GROUP, axis=1)
    auxq_ref[...] = (aux.T if transpose_v else aux).astype(BF16)
    auxk_ref[...] = (-pltpu.roll(aux, AUX_K_LANE - AUX_Q_LANE, axis=1)).astype(BF16)

    for rb in range(n_sub):
        if rb + 1 < n_sub:
            kv_matmul(rb + 1)
        kv_store(rb)


def _shared_kv(x, w_kv, w_f, f_bias, c_in, *, tm, rows_per_seq, n_heads, vt_block):
    rows, d = x.shape
    tm = min(tm, rows_per_seq)
    transpose_v = vt_block is not None
    const = lambda i: (0, 0)
    row_tile = lambda i: (i, 0)
    if transpose_v:
        v_spec = pl.BlockSpec((tm // vt_block, n_heads * V_ROWS, vt_block), lambda i: (i, 0, 0))
        v_shape = jax.ShapeDtypeStruct((rows // vt_block, n_heads * V_ROWS, vt_block), BF16)
        auxq_spec = pl.BlockSpec((LANES, tm), lambda i: (0, i))
        auxq_shape = jax.ShapeDtypeStruct((LANES, rows), BF16)
    else:
        v_spec = pl.BlockSpec((tm, d), row_tile)
        v_shape = jax.ShapeDtypeStruct((rows, d), BF16)
        auxq_spec = pl.BlockSpec((tm, LANES), row_tile)
        auxq_shape = jax.ShapeDtypeStruct((rows, LANES), BF16)
    return pl.pallas_call(
        functools.partial(_kv_kernel, tiles_per_seq=rows_per_seq // tm, n_heads=n_heads,
                          transpose_v=transpose_v),
        grid=(rows // tm,),
        in_specs=[
            pl.BlockSpec((tm, d), row_tile),
            pl.BlockSpec(w_kv.shape, const),
            pl.BlockSpec(w_f.shape, const),
            pl.BlockSpec((1, LANES), const),
            pl.BlockSpec((1, LANES), const),
        ],
        out_specs=[pl.BlockSpec((tm, d), row_tile), v_spec,
                   pl.BlockSpec((tm, LANES), row_tile),
                   auxq_spec,
                   pl.BlockSpec((tm, LANES), row_tile)],
        out_shape=[jax.ShapeDtypeStruct((rows, d), BF16), v_shape,
                   jax.ShapeDtypeStruct((rows, LANES), F32),
                   auxq_shape,
                   jax.ShapeDtypeStruct((rows, LANES), BF16)],
        scratch_shapes=[pltpu.VMEM((1, LANES), F32),
                        pltpu.VMEM((2, min(SUB_ROWS, tm), 2 * d), F32)],
        compiler_params=_params("arbitrary"),
        name="shared_kv",
    )(x, w_kv, w_f, f_bias, c_in)


def _sub_block_pipeline(n_sub, matmul, epilogue):
    matmul(0)
    for rb in range(n_sub):
        if rb + 1 < n_sub:
            matmul(rb + 1)
        epilogue(rb)


def _head_one_hot(h, base, axis):
    shape = (1, LANES) if axis == 1 else (LANES, 1)
    pos = lax.broadcasted_iota(jnp.int32, shape, axis)
    hit = (pos == base + h) | (pos == base + HEAD_GROUP + h) | (pos == base + 2 * HEAD_GROUP + h)
    return jnp.where(hit, 1.0, 0.0).astype(BF16)


def _attn_kernel(qt_ref, auxqt_ref, k_ref, auxk_ref, vt_ref, km_ref, auxkm_ref, vtm_ref, o_ref,
                 qa_sc, s_sc, m_sc, acc_sc, *, tk):
    h = pl.program_id(1)
    qi = pl.program_id(2)
    tq = qt_ref.shape[1]
    ts = tk
    n_strips = tq // ts

    q_sel = _head_one_hot(h, AUX_K_LANE, 0)
    k_sel = _head_one_hot(h, AUX_Q_LANE, 1)
    qa_sc[0:HEAD_DIM, :] = qt_ref[...]
    qa_sc[HEAD_DIM:, :] = auxqt_ref[...] + q_sel
    m_sc[...] = jnp.full(m_sc.shape, -MASK_VALUE, F32)
    acc_sc[...] = jnp.zeros(acc_sc.shape, F32)

    def strip_scores(t, ka, c, causal):
        cs = slice(c * ts, (c + 1) * ts)
        s = jnp.dot(ka, qa_sc[:, cs], preferred_element_type=F32)
        if causal:
            key = lax.broadcasted_iota(jnp.int32, s.shape, 0)
            qry = lax.broadcasted_iota(jnp.int32, s.shape, 1)
            s = jnp.where(key <= qry, s, -MASK_VALUE)
        s_sc[t % SCORE_SLOTS, 0:s.shape[0], :] = s
        return jnp.max(s, axis=0, keepdims=True)

    def strip_update(t, s_max, vt, c):
        cs = slice(c * ts, (c + 1) * ts)
        m = m_sc[:, cs]
        m_new = jnp.maximum(m, s_max)
        corr = jnp.exp2(m - m_new)
        m_sc[:, cs] = m_new
        p = jnp.exp2(s_sc[t % SCORE_SLOTS, 0:vt.shape[1], :] - m_new)
        pv = jnp.dot(vt, p.astype(BF16), preferred_element_type=F32)
        acc_sc[:, cs] = corr * acc_sc[:, cs] + pv

    def run_tasks(tasks):
        pending = [strip_scores(t, ka, c, causal)
                   for t, (ka, _, c, causal) in enumerate(tasks[:QK_LOOKAHEAD])]
        for t, (_, vt, c, _) in enumerate(tasks):
            if t + QK_LOOKAHEAD < len(tasks):
                ka_n, _, c_n, causal_n = tasks[t + QK_LOOKAHEAD]
                pending.append(strip_scores(t + QK_LOOKAHEAD, ka_n, c_n, causal_n))
            strip_update(t, pending.pop(0), vt, c)

    def key_block(j):
        off = pl.multiple_of(j * tk, tk)
        ka = jnp.concatenate([k_ref[pl.ds(off, tk), :], auxk_ref[pl.ds(off, tk), :] + k_sel], axis=1)
        return ka, vt_ref[j]

    def full_blocks(jj, _):
        blocks = [key_block(jj * KV_UNROLL + u) for u in range(KV_UNROLL)]
        run_tasks([(ka, vt, c, False) for ka, vt in blocks for c in range(n_strips)])
        return 0

    assert n_strips % KV_UNROLL == 0
    lax.fori_loop(0, qi * (n_strips // KV_UNROLL), full_blocks, 0)

    diag = [key_block(qi * n_strips + d) for d in range(n_strips)]
    ka_m = jnp.concatenate([km_ref[...], auxkm_ref[...] + k_sel], axis=1)
    run_tasks([(diag[d][0], diag[d][1], c, c == d) for d in range(n_strips) for c in range(d, n_strips)]
              + [(ka_m, vtm_ref[...], c, False) for c in range(n_strips)])

    denom = acc_sc[HEAD_DIM:HEAD_DIM + 1, :]
    o_ref[...] = (acc_sc[0:HEAD_DIM, :] / denom).T.astype(o_ref.dtype)


def _attention(qt, auxqt, k, auxk, vt, km, auxkm, vtm, *, batch, seq, n_heads, tq, tk):
    d, rows = qt.shape
    nq = seq // tq
    nk = seq // tk
    assert vt.shape == (batch * nk, n_heads * V_ROWS, tk)
    return pl.pallas_call(
        functools.partial(_attn_kernel, tk=tk),
        grid=(batch, n_heads, nq),
        in_specs=[
            pl.BlockSpec((HEAD_DIM, tq), lambda b, h, i: (h, b * nq + i)),
            pl.BlockSpec((LANES, tq), lambda b, h, i: (0, b * nq + i)),
            pl.BlockSpec((seq, HEAD_DIM), lambda b, h, i: (b, h)),
            pl.BlockSpec((seq, LANES), lambda b, h, i: (b, 0)),
            pl.BlockSpec((nk, V_ROWS, tk), lambda b, h, i: (b, h, 0)),
            pl.BlockSpec((LANES, HEAD_DIM), lambda b, h, i: (0, h)),
            pl.BlockSpec((LANES, LANES), lambda b, h, i: (0, 0)),
            pl.BlockSpec((V_ROWS, LANES), lambda b, h, i: (h, 0)),
        ],
        out_specs=pl.BlockSpec((tq, HEAD_DIM), lambda b, h, i: (b * nq + i, h)),
        out_shape=jax.ShapeDtypeStruct((rows, d), BF16),
        scratch_shapes=[pltpu.VMEM((HEAD_DIM + LANES, tq), BF16),
                        pltpu.VMEM((SCORE_SLOTS, tk, tk), F32),
                        pltpu.VMEM((1, tq), F32),
                        pltpu.VMEM((V_ROWS, tq), F32)],
        compiler_params=_params("parallel", "parallel", "arbitrary"),
        name="fox_attention",
    )(qt, auxqt, k, auxk, vt, km, auxkm, vtm)


def _oproj_ln_kernel(o_ref, x_ref, w_ref, gain_ref, bias_ref, y_ref, mix_sc, *, alpha):
    tm = x_ref.shape[0]
    sub = min(SUB_ROWS, tm)

    def matmul(rb):
        mix_sc[rb % 2] = jnp.dot(o_ref[rb * sub:(rb + 1) * sub, :], w_ref[...],
                                 preferred_element_type=F32)

    def epilogue(rb):
        rows = slice(rb * sub, (rb + 1) * sub)
        y = x_ref[rows, :].astype(F32) + mix_sc[rb % 2]
        y_ref[rows, :] = _layer_norm(y, gain_ref[...], bias_ref[...], alpha).astype(y_ref.dtype)

    _sub_block_pipeline(tm // sub, matmul, epilogue)


def _oproj_ln(o, x, w, gain, bias, *, alpha, tm):
    rows, d = x.shape
    const = lambda i: (0, 0)
    return pl.pallas_call(
        functools.partial(_oproj_ln_kernel, alpha=alpha),
        grid=(rows // tm,),
        in_specs=[pl.BlockSpec((tm, d), lambda i: (i, 0)),
                  pl.BlockSpec((tm, d), lambda i: (i, 0)),
                  pl.BlockSpec(w.shape, const),
                  pl.BlockSpec((1, d), const),
                  pl.BlockSpec((1, d), const)],
        out_specs=pl.BlockSpec((tm, d), lambda i: (i, 0)),
        out_shape=jax.ShapeDtypeStruct((rows, d), ACT_DTYPE),
        scratch_shapes=[pltpu.VMEM((2, min(SUB_ROWS, tm), d), F32)],
        compiler_params=_params("parallel"),
        name="oproj_ln",
    )(o, x, w, gain, bias)


def kernel(x, meta, ffn1_wg, ffn1_wu, ffn1_wd, ffn2_wg, ffn2_wu, ffn2_wd, ln_gain, ln_bias,
           conv_w_in, conv_w, conv_w_out, kv_w, f_bias, attn_w_q, attn_w_o):
    batch, seq, d = x.shape
    n_meta = meta.shape[0]
    depth = ffn1_wg.shape[0]
    n_heads = f_bias.shape[0]
    assert depth == 2 and conv_w_in.shape[0] == 1 and attn_w_q.shape[0] == 1
    assert d == n_heads * HEAD_DIM and n_heads == HEAD_GROUP
    assert n_meta <= LANES and n_meta % 16 == 0
    alpha = (2 * depth) ** 0.25
    q_scale = LOG2E / math.sqrt(HEAD_DIM)
    assert (batch * seq) % max(FFN_ROWS, CONV_ROWS, KV_ROWS, PROJ_ROWS) == 0
    assert seq % ATTN_QUERY_TILE == 0 and seq % max(CONV_ROWS, KV_ROWS) == 0

    bf = lambda w: w.astype(BF16)
    gain = lambda l, i: ln_gain[l, i].reshape(1, d).astype(F32)
    bias = lambda l, i: ln_bias[l, i].reshape(1, d).astype(F32)
    ffn_w = {1: (ffn1_wg, ffn1_wu, ffn1_wd), 2: (ffn2_wg, ffn2_wu, ffn2_wd)}

    def ffn(h, side, which, l, out_dtype=ACT_DTYPE, **fused_q):
        ln_idx = 0 if which == 1 else 2
        return _ffn_ln(h, side, *ffn_w[which], l, gain(l, ln_idx), bias(l, ln_idx),
                       alpha=alpha, tm=FFN_ROWS, out_dtype=out_dtype, **fused_q)

    xs = x.reshape(batch * seq, d)
    ms = meta.astype(x.dtype)

    w_in, w_out, cw = bf(conv_w_in[0]), bf(conv_w_out[0] / alpha), conv_w[0].astype(F32)
    conv = functools.partial(_conv_ln, alpha=alpha, tm=CONV_ROWS)
    h1, m1 = ffn(xs, ms, 1, 0)
    zero_carry = jnp.zeros((CONV_CARRY_ROWS, d), F32)
    m2, m_tail = conv(m1, w_in, cw, w_out, gain(0, 1), bias(0, 1), zero_carry, rows_per_seq=n_meta)
    h2, _ = conv(h1, w_in, cw, w_out, gain(0, 1), bias(0, 1), m_tail, rows_per_seq=seq)
    h3, m3 = ffn(h2, m2, 2, 0)

    w_kv = bf(kv_w[:, :2 * d])
    w_f = jnp.pad(bf(kv_w[:, 2 * d:]), ((0, 0), (0, LANES - n_heads)))
    fb = jnp.pad(f_bias.astype(F32), (0, LANES - n_heads)).reshape(1, LANES)
    kv = functools.partial(_shared_kv, tm=KV_ROWS, n_heads=n_heads)
    km, vm, cm, _, auxkm = kv(m3, w_kv, w_f, fb, jnp.zeros((1, LANES), F32),
                              rows_per_seq=n_meta, vt_block=None)
    k, vt, _, auxq, auxk = kv(h3, w_kv, w_f, fb, cm[n_meta - 1:n_meta],
                              rows_per_seq=seq, vt_block=ATTN_KEY_BLOCK)
    pad_aux = np.zeros((LANES - n_meta, LANES), np.float32)
    pad_aux[:, AUX_K_LANE:AUX_K_LANE + n_heads] = -MASK_VALUE
    km = jnp.pad(km, ((0, LANES - n_meta), (0, 0)))
    auxkm = jnp.concatenate([auxkm, jnp.asarray(pad_aux, BF16)], axis=0)
    ones_tile = np.zeros((n_heads, BF16_SUBLANES, LANES), np.float32)
    ones_tile[:, 0, :] = 1.0
    vtm = jnp.pad(vm.T, ((0, 0), (0, LANES - n_meta))).reshape(n_heads, HEAD_DIM, LANES)
    vtm = jnp.concatenate([vtm, jnp.asarray(ones_tile, BF16)], axis=1).reshape(n_heads * V_ROWS, LANES)

    h4, q = ffn(h3, None, 1, 1, wq=bf(attn_w_q[0]), q_scale=q_scale)
    o = _attention(q, auxq, k, auxk, vt, km, auxkm, vtm, batch=batch, seq=seq, n_heads=n_heads,
                   tq=ATTN_QUERY_TILE, tk=ATTN_KEY_BLOCK)
    h5 = _oproj_ln(o, h4, bf(attn_w_o[0] / alpha), gain(1, 1), bias(1, 1), alpha=alpha, tm=PROJ_ROWS)
    out, _ = ffn(h5, None, 2, 1, out_dtype=x.dtype)
    return out.reshape(batch, seq, d)
```
